```python
import jax, jax.numpy as jnp
from jax import lax
import numpy as np

D_MODEL = 1024
BATCH = 2
SEQ = 8192
DEPTH = 1
DEC_BATCH = 128
DEC_SEQ = 8
PAST_LEN = 2048
PAGE_SIZE = 128

CONV_WIDTH = D_MODEL // 2
SB_HEADS = 8
SB_HEAD_DIM = 64
SB_WIDTH = SB_HEADS * SB_HEAD_DIM
KSIZE = 3
D_FF = 2816
PLE_DIM = 256
Q_BLOCK = 128
EPS = 1e-6
SB_BIAS_INIT = -6.0
IN_SIZES = [CONV_WIDTH] * 3 + [SB_WIDTH] * 3 + [D_MODEL] * 2
N_IN = sum(IN_SIZES)
IN_SPLITS = np.cumsum(IN_SIZES)[:-1].tolist()

kernel_name = "gated_shortconv_stickbreaking_convffn_decode_step"


def rms_norm(x, g):
    xf = x.astype(jnp.float32)
    y = xf * lax.rsqrt(jnp.mean(xf * xf, axis=-1, keepdims=True) + EPS)
    return (y * g.astype(jnp.float32)).astype(x.dtype)


def causal_dwconv(u, prev, w, b=None):
    t = u.shape[1]
    full = jnp.concatenate([prev.astype(u.dtype), u], axis=1)
    y = w[0] * full[:, 0:t]
    for j in range(1, KSIZE):
        y = y + w[j] * full[:, j:j + t]
    if b is not None:
        y = y + b
    return y, full[:, t:]


def sb_block(q, k, v, bias, q_pos, k_pos):
    z = jnp.einsum('bqhd,bkhd->bhqk', q.astype(jnp.float32), k.astype(jnp.float32)) * (SB_HEAD_DIM ** -0.5)
    z = z + bias.astype(jnp.float32)[None, :, None, None]
    mask = k_pos[None, :] < q_pos[:, None]
    log_beta = jax.nn.log_sigmoid(z)
    log_keep = jnp.where(mask, jax.nn.log_sigmoid(-z), 0.0)
    later = lax.cumsum(log_keep, axis=3, reverse=True) - log_keep
    a = jnp.where(mask, jnp.exp(log_beta + later), 0.0)
    return jnp.einsum('bhqk,bkhd->bqhd', a, v.astype(jnp.float32))


def stick_breaking(q, k, v, bias):
    t = q.shape[1]
    off = k.shape[1] - t
    outs = []
    for s in range(0, t, Q_BLOCK):
        e = min(s + Q_BLOCK, t)
        kend = off + e
        q_pos = off + jnp.arange(s, e)
        k_pos = jnp.arange(kend)
        outs.append(sb_block(q[:, s:e], k[:, :kend], v[:, :kend], bias, q_pos, k_pos))
    return jnp.concatenate(outs, axis=1)


def trunk_layer(x, p, k_past, v_past, conv_prev, ffn_prev,
                g_pre_mix, w_in, w_conv_a, w_a, w_b, sb_bias, w_o, g_post_mix,
                g_pre_ffn, w_up, w_conv_f, b_conv_f, w_down, g_post_ffn,
                w_ple, g_ple_gate, w_ple_gate, g_post_ple):
    bsz, t, _ = x.shape
    h = rms_norm(x, g_pre_mix)
    cb, cc, ch, q, k, v, ga, gb = jnp.split(h @ w_in, IN_SPLITS, axis=-1)
    conv_u, conv_state = causal_dwconv(cc * ch, conv_prev, w_conv_a)
    y_a = (cb * conv_u) @ w_a
    q = q.reshape(bsz, t, SB_HEADS, SB_HEAD_DIM)
    k = k.reshape(bsz, t, SB_HEADS, SB_HEAD_DIM)
    v = v.reshape(bsz, t, SB_HEADS, SB_HEAD_DIM)
    k_all = jnp.concatenate([k_past.astype(k.dtype), k], axis=1)
    v_all = jnp.concatenate([v_past.astype(v.dtype), v], axis=1)
    o = stick_breaking(q, k_all, v_all, sb_bias).astype(x.dtype).reshape(bsz, t, SB_WIDTH)
    y_b = o @ w_b
    merged = jax.nn.sigmoid(ga) * y_a + jax.nn.sigmoid(gb) * y_b
    x = x + rms_norm(merged @ w_o, g_post_mix)
    up, ffn_state = causal_dwconv(rms_norm(x, g_pre_ffn) @ w_up, ffn_prev, w_conv_f, b_conv_f)
    gate, val = jnp.split(up, 2, axis=-1)
    x = x + rms_norm((jax.nn.gelu(gate) * val) @ w_down, g_post_ffn)
    e = (p @ w_ple) * jax.nn.sigmoid(rms_norm(x, g_ple_gate) @ w_ple_gate)
    x = x + rms_norm(e, g_post_ple)
    return x, k, v, conv_state, ffn_state


def setup_inputs(seed: int = 0) -> dict:
    key = jax.random.key(seed)
    ks = iter(jax.random.split(key, 40))
    f32 = jnp.float32
    n_pages = PAST_LEN // PAGE_SIZE
    n_used = DEC_BATCH * n_pages
    n_pool = n_used + n_used // 4

    def nrm(shape, scale=1.0):
        return jax.random.normal(next(ks), shape, f32) * scale

    def gain(shape):
        return 1.0 + 0.05 * jax.random.normal(next(ks), shape, f32)

    x_prompt = nrm((BATCH, SEQ, D_MODEL))
    x_sample = nrm((DEC_BATCH, DEC_SEQ, D_MODEL))
    cache_k = nrm((DEPTH, n_pool, PAGE_SIZE, SB_HEADS, SB_HEAD_DIM))
    cache_v = nrm((DEPTH, n_pool, PAGE_SIZE, SB_HEADS, SB_HEAD_DIM))
    state_conv = nrm((DEPTH, DEC_BATCH, KSIZE - 1, CONV_WIDTH))
    state_ffn_conv = nrm((DEPTH, DEC_BATCH, KSIZE - 1, 2 * D_FF))
    page_table = jax.random.permutation(next(ks), n_pool)[:n_used].reshape(DEC_BATCH, n_pages).astype(jnp.int32)
    p_prompt = nrm((DEPTH, BATCH, SEQ, PLE_DIM))
    p_sample = nrm((DEPTH, DEC_BATCH, DEC_SEQ, PLE_DIM))
    return {
        "x_prompt": x_prompt, "x_sample": x_sample,
        "cache_k": cache_k, "cache_v": cache_v,
        "state_conv": state_conv, "state_ffn_conv": state_ffn_conv,
        "page_table": page_table, "p_prompt": p_prompt, "p_sample": p_sample,
        "g_pre_mix": gain((DEPTH, D_MODEL)),
        "w_in": nrm((DEPTH, D_MODEL, N_IN), D_MODEL ** -0.5),
        "w_conv_a": nrm((DEPTH, KSIZE, CONV_WIDTH), KSIZE ** -0.5),
        "w_a": nrm((DEPTH, CONV_WIDTH, D_MODEL), CONV_WIDTH ** -0.5),
        "w_b": nrm((DEPTH, SB_WIDTH, D_MODEL), SB_WIDTH ** -0.5),
        "sb_bias": SB_BIAS_INIT + nrm((DEPTH, SB_HEADS), 0.1),
        "w_o": nrm((DEPTH, D_MODEL, D_MODEL), D_MODEL ** -0.5),
        "g_post_mix": gain((DEPTH, D_MODEL)),
        "g_pre_ffn": gain((DEPTH, D_MODEL)),
        "w_up": nrm((DEPTH, D_MODEL, 2 * D_FF), D_MODEL ** -0.5),
        "w_conv_f": nrm((DEPTH, KSIZE, 2 * D_FF), KSIZE ** -0.5),
        "b_conv_f": nrm((DEPTH, 2 * D_FF), 0.01),
        "w_down": nrm((DEPTH, D_FF, D_MODEL), D_FF ** -0.5),
        "g_post_ffn": gain((DEPTH, D_MODEL)),
        "w_ple": nrm((DEPTH, PLE_DIM, D_MODEL), PLE_DIM ** -0.5),
        "g_ple_gate": gain((DEPTH, D_MODEL)),
        "w_ple_gate": nrm((DEPTH, D_MODEL, D_MODEL), D_MODEL ** -0.5),
        "g_post_ple": gain((DEPTH, D_MODEL)),
    }


def reference(x_prompt, x_sample, cache_k, cache_v, state_conv, state_ffn_conv, page_table,
              p_prompt, p_sample, g_pre_mix, w_in, w_conv_a, w_a, w_b, sb_bias, w_o, g_post_mix,
              g_pre_ffn, w_up, w_conv_f, b_conv_f, w_down, g_post_ffn,
              w_ple, g_ple_gate, w_ple_gate, g_post_ple):
    n_pages = PAST_LEN // PAGE_SIZE
    nb_p = x_prompt.shape[0]
    nb_s = x_sample.shape[0]
    xp, xs = x_prompt, x_sample
    kp_l, vp_l, cp_l, fp_l = [], [], [], []
    ks_l, vs_l, cs_l, fs_l = [], [], [], []
    for i in range(DEPTH):
        lw = (g_pre_mix[i], w_in[i], w_conv_a[i], w_a[i], w_b[i], sb_bias[i], w_o[i], g_post_mix[i],
              g_pre_ffn[i], w_up[i], w_conv_f[i], b_conv_f[i], w_down[i], g_post_ffn[i],
              w_ple[i], g_ple_gate[i], w_ple_gate[i], g_post_ple[i])
        empty_kv = jnp.zeros((nb_p, 0, SB_HEADS, SB_HEAD_DIM), xp.dtype)
        xp, kp, vp, cp, fp = trunk_layer(
            xp, p_prompt[i], empty_kv, empty_kv,
            jnp.zeros((nb_p, KSIZE - 1, CONV_WIDTH), xp.dtype),
            jnp.zeros((nb_p, KSIZE - 1, 2 * D_FF), xp.dtype), *lw)
        k_past = cache_k[i][page_table].reshape(nb_s, n_pages * PAGE_SIZE, SB_HEADS, SB_HEAD_DIM)
        v_past = cache_v[i][page_table].reshape(nb_s, n_pages * PAGE_SIZE, SB_HEADS, SB_HEAD_DIM)
        xs, ksm, vsm, csm, fsm = trunk_layer(
            xs, p_sample[i], k_past, v_past, state_conv[i], state_ffn_conv[i], *lw)
        kp_l.append(kp); vp_l.append(vp); cp_l.append(cp); fp_l.append(fp)
        ks_l.append(ksm); vs_l.append(vsm); cs_l.append(csm); fs_l.append(fsm)
    return (xp, xs,
            jnp.stack(kp_l), jnp.stack(vp_l), jnp.stack(cp_l), jnp.stack(fp_l),
            jnp.stack(ks_l), jnp.stack(vs_l), jnp.stack(cs_l), jnp.stack(fs_l))
```

```python
import functools

import jax
import jax.numpy as jnp
from jax import lax
from jax.experimental import pallas as pl
from jax.experimental.pallas import tpu as pltpu

F32 = jnp.float32
BF16 = jnp.bfloat16

EPS = 1e-6
HEADS = 8
HEAD_DIM = 64
SB_WIDTH = HEADS * HEAD_DIM
KSIZE = 3

LANES = 128
SUBLANES = 8
ROW_TILE = 512
HIST_OUT_ROW_TILE = 256
KEY_BLOCK = 256
FF_CHUNK = 256
VMEM_LIMIT = 56 * 1024 * 1024

_NT = (((1,), (1,)), ((), ()))


def _rms(x, g):
    return x * lax.rsqrt(jnp.mean(x * x, axis=-1, keepdims=True) + EPS) * g


def _sigmoid(x):
    return 1.0 / (1.0 + jnp.exp(-x))


def _gelu_tanh(x):
    return x * (0.5 * (1.0 + jnp.tanh(0.7978845608028654 * (x + 0.044715 * (x * x * x)))))


def _select_rows(e, x):
    hi = x.astype(BF16)
    lo = (x - hi.astype(F32)).astype(BF16)
    return (jnp.dot(e, hi, preferred_element_type=F32)
            + jnp.dot(e, lo, preferred_element_type=F32))


def _causal_conv3(u, scr, w, prev, hist):
    tm = u.shape[0]
    scr[0:SUBLANES, :] = jnp.zeros((SUBLANES, u.shape[1]), F32) if prev is None else prev
    scr[SUBLANES:tm + SUBLANES, :] = u
    s1 = scr[SUBLANES - 1:tm + SUBLANES - 1, :]
    s2 = scr[SUBLANES - 2:tm + SUBLANES - 2, :]
    if hist is not None:
        h1, h2, tmod = hist
        s1 = jnp.where(tmod == 0, h1, s1)
        s2 = jnp.where(tmod < 2, h2, s2)
    return w[0:1, :] * s2 + w[1:2, :] * s1 + w[2:3, :] * u


def _mixer_in_kernel(*refs, hist_mode):
    if hist_mode:
        (x_ref, g_ref, win_ref, wca_ref, wa_ref, e1_ref, e2_ref, st_ref,
         q_ref, kf_ref, vf_ref, ga_ref, sgb_ref, u_ref, scr) = refs
    else:
        (x_ref, g_ref, win_ref, wca_ref, wa_ref,
         q_ref, k16_ref, v16_ref, kf_ref, vf_ref, ga_ref, sgb_ref, cst_ref, scr, tail_scr) = refs
    tm = x_ref.shape[0]
    h = _rms(x_ref[...], g_ref[...]).astype(BF16)

    def proj(lo, hi):
        return jnp.dot(h, win_ref[:, lo:hi], preferred_element_type=F32)

    cw = SB_WIDTH
    u = proj(cw, 2 * cw) * proj(2 * cw, 3 * cw)
    if hist_mode:
        st = st_ref[...]
        tmod = lax.broadcasted_iota(jnp.int32, u.shape, 0) & (SUBLANES - 1)
        hist = (_select_rows(e1_ref[...], st), _select_rows(e2_ref[...], st), tmod)
        prev = None
        u_ref[...] = u
    else:
        hist = None

        @pl.when(pl.program_id(1) == 0)
        def _():
            tail_scr[...] = jnp.zeros(tail_scr.shape, F32)

        prev = tail_scr[...]
        tail_scr[...] = u[tm - SUBLANES:tm, :]
        cst_ref[...] = u[tm - (KSIZE - 1):tm, :]
    conv = _causal_conv3(u, scr, wca_ref[...], prev, hist)
    ya = jnp.dot((proj(0, cw) * conv).astype(BF16), wa_ref[...], preferred_element_type=F32)

    q = proj(3 * cw, 4 * cw) * (HEAD_DIM ** -0.5)
    q_ref[...] = q.astype(q_ref.dtype)
    k = proj(4 * cw, 5 * cw)
    kf_ref[...] = k
    v = proj(5 * cw, 6 * cw)
    vf_ref[...] = v
    if not hist_mode:
        k16_ref[...] = k.astype(BF16)
        v16_ref[...] = v.astype(BF16)
    d = ga_ref.shape[1]
    ga_ref[...] = _sigmoid(proj(6 * cw, 6 * cw + d)) * ya
    sgb_ref[...] = _sigmoid(proj(6 * cw + d, 6 * cw + 2 * d))


def _mixer_in(x, g, w_in, w_conv_a, w_a, hist):
    nb, t, d = x.shape
    tm = min(ROW_TILE, t)
    nt = t // tm
    cw = SB_WIDTH
    hist_mode = hist is not None

    def row_spec(c):
        return pl.BlockSpec((None, tm, c), lambda b, i: (b, i, 0))

    def const_spec(shape):
        return pl.BlockSpec(shape, lambda b, i: (0,) * len(shape), pipeline_mode=pl.Buffered(1))

    in_specs = [row_spec(d), const_spec((1, d)), const_spec(w_in.shape),
                const_spec(w_conv_a.shape), const_spec(w_a.shape)]
    args = [x, g, w_in, w_conv_a, w_a]
    rows = lambda c, dt: jax.ShapeDtypeStruct((nb, t, c), dt)
    if hist_mode:
        e1, e2, st = hist
        in_specs += [const_spec(e1.shape), const_spec(e2.shape),
                     pl.BlockSpec((None, tm // 4, cw), lambda b, i: (b, i, 0))]
        args += [e1, e2, st]
        out_shape = [rows(cw, F32), rows(cw, F32), rows(cw, F32), rows(d, F32), rows(d, F32),
                     rows(cw, F32)]
        out_specs = [row_spec(cw)] * 3 + [row_spec(d)] * 2 + [row_spec(cw)]
    else:
        out_shape = [rows(cw, BF16), rows(cw, BF16), rows(cw, BF16), rows(cw, F32), rows(cw, F32),
                     rows(d, F32), rows(d, F32),
                     jax.ShapeDtypeStruct((nb, KSIZE - 1, cw), F32)]
        out_specs = ([row_spec(cw)] * 5 + [row_spec(d)] * 2
                     + [pl.BlockSpec((None, KSIZE - 1, cw), lambda b, i: (b, 0, 0))])
    return pl.pallas_call(
        functools.partial(_mixer_in_kernel, hist_mode=hist_mode),
        grid=(nb, nt),
        in_specs=in_specs,
        out_specs=out_specs,
        out_shape=out_shape,
        scratch_shapes=[pltpu.VMEM((tm + SUBLANES, cw), F32)]
        + ([] if hist_mode else [pltpu.VMEM((SUBLANES, cw), F32)]),
        compiler_params=pltpu.CompilerParams(
            dimension_semantics=("arbitrary", "arbitrary"), vmem_limit_bytes=VMEM_LIMIT),
        name="mixer_in_hist" if hist_mode else "mixer_in",
    )(*args)


def _sb_block(z, mask, w2, v_blk, carry, acc):
    n = acc.shape[1]
    sp = jnp.maximum(z, 0.0) + jnp.log(1.0 + jnp.exp(-jnp.abs(z)))
    spm = sp if mask is None else jnp.where(mask, sp, 0.0)
    hi = spm.astype(BF16)
    lo = (spm - hi.astype(F32)).astype(BF16)
    s = jnp.dot(jnp.concatenate([hi, lo], axis=1), w2, preferred_element_type=F32)
    later = s[:, :KEY_BLOCK]
    total = s[:, KEY_BLOCK:]
    a = jnp.exp(z - sp - later)
    if mask is not None:
        a = jnp.where(mask, a, 0.0)
    pv = jnp.dot(a.astype(BF16), v_blk, preferred_element_type=F32)
    scale = jnp.exp(-carry)
    if n != LANES:
        scale = jnp.concatenate([scale] * (n // LANES), axis=1)
    return carry + total, acc + scale * pv


def _attn_prompt_kernel(bias_ref, q_ref, k_ref, v_ref, w2_ref, o_ref):
    tq = q_ref.shape[0]
    hp = pl.program_id(1)
    qi = pl.program_id(2)
    q2 = q_ref[...].astype(F32)
    lane = lax.broadcasted_iota(jnp.int32, (tq, LANES), 1)
    row = lax.broadcasted_iota(jnp.int32, (tq, KEY_BLOCK), 0)
    col = lax.broadcasted_iota(jnp.int32, (tq, KEY_BLOCK), 1)
    diag_mask = col < row
    w2 = w2_ref[...]
    halves = []
    for hh in range(2):
        in_half = (lane < HEAD_DIM) if hh == 0 else (lane >= HEAD_DIM)
        qm = jnp.where(in_half, q2, 0.0).astype(BF16)
        b = bias_ref[2 * hp + hh]

        def block(j, carry, acc, mask, qm=qm, b=b):
            start = pl.multiple_of(j * KEY_BLOCK, KEY_BLOCK)
            kb = k_ref[pl.ds(start, KEY_BLOCK), :]
            vb = v_ref[pl.ds(start, KEY_BLOCK), :]
            z = lax.dot_general(qm, kb, _NT, preferred_element_type=F32) + b
            return _sb_block(z, mask, w2, vb, carry, acc)

        zero = jnp.zeros((tq, LANES), F32)
        carry, acc = block(qi, zero, zero, diag_mask)

        def body(jj, c, block=block):
            return block(qi - 1 - jj, c[0], c[1], None)

        carry, acc = lax.fori_loop(0, qi, body, (carry, acc))
        halves.append(acc)
    o_ref[...] = jnp.where(lane < HEAD_DIM, halves[0], halves[1]).astype(o_ref.dtype)


def _attn_prompt(q, k, v, sb_bias, w2):
    nb, t, _ = q.shape
    tq = KEY_BLOCK
    npair = SB_WIDTH // LANES
    return pl.pallas_call(
        _attn_prompt_kernel,
        grid=(nb, npair, t // tq),
        in_specs=[
            pl.BlockSpec(memory_space=pltpu.SMEM),
            pl.BlockSpec((None, tq, LANES), lambda b, h, i: (b, i, h)),
            pl.BlockSpec((None, t, LANES), lambda b, h, i: (b, 0, h)),
            pl.BlockSpec((None, t, LANES), lambda b, h, i: (b, 0, h)),
            pl.BlockSpec(w2.shape, lambda b, h, i: (0, 0), pipeline_mode=pl.Buffered(1)),
        ],
        out_specs=pl.BlockSpec((None, tq, LANES), lambda b, h, i: (b, i, h)),
        out_shape=jax.ShapeDtypeStruct((nb, t, SB_WIDTH), BF16),
        compiler_params=pltpu.CompilerParams(
            dimension_semantics=("arbitrary", "arbitrary", "arbitrary"),
            vmem_limit_bytes=VMEM_LIMIT),
        name="attn_prompt",
    )(sb_bias, q, k, v, w2)


def _attn_sample_kernel(pt_ref, q_ref, kn_ref, vn_ref, bias_ref, bd_ref, w2_ref, *rest, n_pages):
    del pt_ref
    k_pages = rest[:n_pages]
    v_pages = rest[n_pages:2 * n_pages]
    o_ref = rest[2 * n_pages]
    t, width = q_ref.shape
    rows = HEADS * t
    bd = bd_ref[...]
    qbd = (jnp.concatenate([q_ref[...]] * HEADS, axis=0) * bd).astype(BF16)
    bias = jnp.concatenate([bias_ref[...]] * (KEY_BLOCK // LANES), axis=1)
    w2 = w2_ref[...]
    pad = jnp.zeros((KEY_BLOCK - t, width), F32)
    row = lax.broadcasted_iota(jnp.int32, (rows, KEY_BLOCK), 0)
    col = lax.broadcasted_iota(jnp.int32, (rows, KEY_BLOCK), 1)
    new_mask = col < (row & (t - 1))

    def block(kb, vb, carry, acc, mask):
        z = lax.dot_general(qbd, kb.astype(BF16), _NT, preferred_element_type=F32) + bias
        return _sb_block(z, mask, w2, vb.astype(BF16), carry, acc)

    carry = jnp.zeros((rows, LANES), F32)
    acc = jnp.zeros((rows, width), F32)
    carry, acc = block(jnp.concatenate([kn_ref[...], pad], axis=0),
                       jnp.concatenate([vn_ref[...], pad], axis=0), carry, acc, new_mask)
    per_block = KEY_BLOCK // k_pages[0].shape[0]
    for j in reversed(range(n_pages // per_block)):
        kb = jnp.concatenate([k_pages[j * per_block + r][...] for r in range(per_block)], axis=0)
        vb = jnp.concatenate([v_pages[j * per_block + r][...] for r in range(per_block)], axis=0)
        carry, acc = block(kb, vb, carry, acc, None)
    sel = acc * bd
    out = sel[0:t, :]
    for hd in range(1, HEADS):
        out = out + sel[hd * t:(hd + 1) * t, :]
    o_ref[...] = out


def _attn_sample(q, k_new, v_new, cache_k, cache_v, page_table, bias_rows, bd, w2):
    nb, t, width = q.shape
    n_pages = page_table.shape[1]
    page = cache_k.shape[1]
    tok = pl.BlockSpec((None, t, width), lambda b, pt: (b, 0, 0))

    def const(shape):
        return pl.BlockSpec(shape, lambda b, pt: (0,) * len(shape), pipeline_mode=pl.Buffered(1))

    def page_spec(j):
        return pl.BlockSpec((None, page, width), lambda b, pt, j=j: (pt[b * n_pages + j], 0, 0))

    grid_spec = pltpu.PrefetchScalarGridSpec(
        num_scalar_prefetch=1,
        grid=(nb,),
        in_specs=[tok, tok, tok, const(bias_rows.shape), const(bd.shape), const(w2.shape)]
        + [page_spec(j) for j in range(n_pages)] * 2,
        out_specs=tok,
    )
    return pl.pallas_call(
        functools.partial(_attn_sample_kernel, n_pages=n_pages),
        grid_spec=grid_spec,
        out_shape=jax.ShapeDtypeStruct((nb, t, width), F32),
        compiler_params=pltpu.CompilerParams(
            dimension_semantics=("arbitrary",), vmem_limit_bytes=VMEM_LIMIT),
        name="attn_sample",
    )(page_table.reshape(-1), q, k_new, v_new, bias_rows, bd, w2,
      *([cache_k] * n_pages), *([cache_v] * n_pages))


def _mixer_out_kernel(*refs, hist_mode):
    if hist_mode:
        (x_ref, ga_ref, sgb_ref, o_ref, p_ref, wb_ref, wo_ref, gpm_ref, gpf_ref, wup_ref,
         wcf_ref, bcf_ref, wdn_ref, gqf_ref, wple_ref, gpg_ref, wpg_ref, gpp_ref,
         e1_ref, e2_ref, st_ref, y_ref, up_ref, dn_scr, c_scr) = refs
        tail_scr = None
    else:
        (x_ref, ga_ref, sgb_ref, o_ref, p_ref, wb_ref, wo_ref, gpm_ref, gpf_ref, wup_ref,
         wcf_ref, bcf_ref, wdn_ref, gqf_ref, wple_ref, gpg_ref, wpg_ref, gpp_ref,
         y_ref, fst_ref, dn_scr, c_scr, tail_scr) = refs
    tm = x_ref.shape[0]
    dff = wdn_ref.shape[0]
    yb = jnp.dot(o_ref[...].astype(BF16), wb_ref[...], preferred_element_type=F32)
    merged = ga_ref[...] + sgb_ref[...] * yb
    x1 = x_ref[...] + _rms(jnp.dot(merged.astype(BF16), wo_ref[...],
                                   preferred_element_type=F32), gpm_ref[...])
    h2 = _rms(x1, gpf_ref[...]).astype(BF16)

    if hist_mode:
        tmod = lax.broadcasted_iota(jnp.int32, (tm, FF_CHUNK), 0) & (SUBLANES - 1)
        e1 = e1_ref[...]
        e2 = e2_ref[...]
    else:
        @pl.when(pl.program_id(1) == 0)
        def _():
            tail_scr[...] = jnp.zeros(tail_scr.shape, F32)

    def conv_part(lo):
        hi = lo + FF_CHUNK
        up = jnp.dot(h2, wup_ref[:, lo:hi], preferred_element_type=F32)
        if hist_mode:
            st = st_ref[:, lo:hi]
            hist = (_select_rows(e1, st), _select_rows(e2, st), tmod)
            prev = None
            up_ref[:, lo:hi] = up
        else:
            hist = None
            prev = tail_scr[:, lo:hi]
            tail_scr[:, lo:hi] = up[tm - SUBLANES:tm, :]
            fst_ref[:, lo:hi] = up[tm - (KSIZE - 1):tm, :]
        return _causal_conv3(up, c_scr, wcf_ref[:, lo:hi], prev, hist) + bcf_ref[:, lo:hi]

    for c in range(dff // FF_CHUNK):
        gate = conv_part(c * FF_CHUNK)
        val = conv_part(dff + c * FF_CHUNK)
        part = jnp.dot((_gelu_tanh(gate) * val).astype(BF16),
                       wdn_ref[c * FF_CHUNK:(c + 1) * FF_CHUNK, :], preferred_element_type=F32)
        if c == 0:
            dn_scr[...] = part
        else:
            dn_scr[...] += part
    x2 = x1 + _rms(dn_scr[...], gqf_ref[...])
    gate = _sigmoid(jnp.dot(_rms(x2, gpg_ref[...]).astype(BF16), wpg_ref[...],
                            preferred_element_type=F32))
    e = jnp.dot(p_ref[...].astype(BF16), wple_ref[...], preferred_element_type=F32) * gate
    y_ref[...] = x2 + _rms(e, gpp_ref[...])


def _mixer_out(x, ga, sgb, o, p, wb, wo, gpm, gpf, wup, wcf, bcf, wdn, gqf, wple, gpg, wpg, gpp,
               hist):
    nb, t, d = x.shape
    hist_mode = hist is not None
    tm = min(HIST_OUT_ROW_TILE if hist_mode else ROW_TILE, t)
    nt = t // tm
    dff2 = wup.shape[1]

    def row_spec(c):
        return pl.BlockSpec((None, tm, c), lambda b, i: (b, i, 0))

    def const_spec(shape):
        return pl.BlockSpec(shape, lambda b, i: (0,) * len(shape), pipeline_mode=pl.Buffered(1))

    consts = [wb, wo, gpm, gpf, wup, wcf, bcf, wdn, gqf, wple, gpg, wpg, gpp]
    in_specs = ([row_spec(d), row_spec(d), row_spec(d), row_spec(o.shape[2]), row_spec(p.shape[2])]
                + [const_spec(c.shape) for c in consts])
    args = [x, ga, sgb, o, p] + consts
    scratch = [pltpu.VMEM((tm, d), F32), pltpu.VMEM((tm + SUBLANES, FF_CHUNK), F32)]
    if hist_mode:
        e1, e2, st = hist
        in_specs += [const_spec(e1.shape), const_spec(e2.shape),
                     pl.BlockSpec((None, tm // 4, dff2), lambda b, i: (b, i, 0))]
        args += [e1, e2, st]
        out_shape = [jax.ShapeDtypeStruct((nb, t, d), F32),
                     jax.ShapeDtypeStruct((nb, t, dff2), F32)]
        out_specs = [row_spec(d), row_spec(dff2)]
    else:
        out_shape = [jax.ShapeDtypeStruct((nb, t, d), F32),
                     jax.ShapeDtypeStruct((nb, KSIZE - 1, dff2), F32)]
        out_specs = [row_spec(d), pl.BlockSpec((None, KSIZE - 1, dff2), lambda b, i: (b, 0, 0))]
        scratch.append(pltpu.VMEM((SUBLANES, dff2), F32))
    return pl.pallas_call(
        functools.partial(_mixer_out_kernel, hist_mode=hist_mode),
        grid=(nb, nt),
        in_specs=in_specs,
        out_specs=out_specs,
        out_shape=out_shape,
        scratch_shapes=scratch,
        compiler_params=pltpu.CompilerParams(
            dimension_semantics=("arbitrary", "arbitrary"), vmem_limit_bytes=VMEM_LIMIT),
        name="mixer_out_hist" if hist_mode else "mixer_out",
    )(*args)


def _suffix_sum_matrix():
    j = lax.broadcasted_iota(jnp.int32, (KEY_BLOCK, KEY_BLOCK + LANES), 0)
    s = lax.broadcasted_iota(jnp.int32, (KEY_BLOCK, KEY_BLOCK + LANES), 1)
    w = jnp.where((j > s) | (s >= KEY_BLOCK), 1.0, 0.0).astype(BF16)
    return jnp.concatenate([w, w], axis=0)


def _history_selectors(tm, t):
    r = lax.broadcasted_iota(jnp.int32, (tm, tm // 4), 0)
    c = lax.broadcasted_iota(jnp.int32, (tm, tm // 4), 1)
    seq, pos = r // t, r % t
    e1 = ((pos == 0) & (c == 2 * seq + 1)).astype(BF16)
    e2 = (((pos == 0) & (c == 2 * seq)) | ((pos == 1) & (c == 2 * seq + 1))).astype(BF16)
    return e1, e2


def kernel(x_prompt, x_sample, cache_k, cache_v, state_conv, state_ffn_conv, page_table, p_prompt, p_sample, g_pre_mix, w_in, w_conv_a, w_a, w_b, sb_bias, w_o, g_post_mix, g_pre_ffn, w_up, w_conv_f, b_conv_f, w_down, g_post_ffn, w_ple, g_ple_gate, w_ple_gate, g_post_ple):
    depth = w_in.shape[0]
    nbp, seq, d = x_prompt.shape
    nbs, dseq, _ = x_sample.shape
    assert dseq == SUBLANES and KSIZE == 3
    n_pool, page = cache_k.shape[1], cache_k.shape[2]
    rows_s = nbs * dseq
    w2 = _suffix_sum_matrix()
    sel_in = _history_selectors(min(ROW_TILE, rows_s), dseq)
    sel_out = _history_selectors(min(HIST_OUT_ROW_TILE, rows_s), dseq)
    r64 = lax.broadcasted_iota(jnp.int32, (HEADS * dseq, SB_WIDTH), 0) // dseq
    c64 = lax.broadcasted_iota(jnp.int32, (HEADS * dseq, SB_WIDTH), 1) // HEAD_DIM
    bd = (r64 == c64).astype(F32)
    row2 = lambda a: a.reshape(1, -1)

    xp = x_prompt
    xs = x_sample.reshape(1, rows_s, d)
    outs = [[] for _ in range(8)]
    for i in range(depth):
        win, wa, wb, wo = (w.astype(BF16) for w in (w_in[i], w_a[i], w_b[i], w_o[i]))
        wup, wdn, wple, wpg = (w.astype(BF16) for w in (w_up[i], w_down[i], w_ple[i], w_ple_gate[i]))
        tail = (wb, wo, row2(g_post_mix[i]), row2(g_pre_ffn[i]), wup, w_conv_f[i],
                row2(b_conv_f[i]), wdn, row2(g_post_ffn[i]), wple, row2(g_ple_gate[i]), wpg,
                row2(g_post_ple[i]))
        q, k16, v16, kf, vf, ga, sgb, cst = _mixer_in(
            xp, row2(g_pre_mix[i]), win, w_conv_a[i], wa, None)
        o = _attn_prompt(q, k16, v16, sb_bias[i], w2)
        xp, fst = _mixer_out(xp, ga, sgb, o, p_prompt[i], *tail, None)
        outs[0].append(kf.reshape(nbp, seq, HEADS, HEAD_DIM))
        outs[1].append(vf.reshape(nbp, seq, HEADS, HEAD_DIM))
        outs[2].append(cst)
        outs[3].append(fst)
        st_a = state_conv[i].reshape(1, nbs * (KSIZE - 1), -1)
        st_f = state_ffn_conv[i].reshape(1, nbs * (KSIZE - 1), -1)
        q, kf, vf, ga, sgb, u = _mixer_in(
            xs, row2(g_pre_mix[i]), win, w_conv_a[i], wa, (*sel_in, st_a))
        bias_rows = jnp.broadcast_to(
            jnp.repeat(sb_bias[i].astype(F32), dseq)[:, None], (HEADS * dseq, LANES))
        o = _attn_sample(q.reshape(nbs, dseq, -1), kf.reshape(nbs, dseq, -1),
                         vf.reshape(nbs, dseq, -1),
                         cache_k[i].reshape(n_pool, page, -1), cache_v[i].reshape(n_pool, page, -1),
                         page_table, bias_rows, bd, w2)
        xs, up = _mixer_out(xs, ga, sgb, o.reshape(1, rows_s, -1),
                            p_sample[i].reshape(1, rows_s, -1), *tail, (*sel_out, st_f))
        outs[4].append(kf.reshape(nbs, dseq, HEADS, HEAD_DIM))
        outs[5].append(vf.reshape(nbs, dseq, HEADS, HEAD_DIM))
        outs[6].append(u.reshape(nbs, dseq, -1)[:, dseq - (KSIZE - 1):])
        outs[7].append(up.reshape(nbs, dseq, -1)[:, dseq - (KSIZE - 1):])
    stacked = [jnp.stack(o) for o in outs]
    return (xp, xs.reshape(nbs, dseq, d), *stacked)
```

```python
import functools

import jax
import jax.numpy as jnp
from jax import lax
from jax.experimental import pallas as pl
from jax.experimental.pallas import tpu as pltpu

F32 = jnp.float32
BF16 = jnp.bfloat16

EPS = 1e-6
HEADS = 8
HEAD_DIM = 64
SB_WIDTH = HEADS * HEAD_DIM
KSIZE = 3

LANES = 128
SUBLANES = 8
ROW_TILE = 512
HIST_OUT_ROW_TILE = 256
MXU_COLS = 256
KEY_BLOCK = 256
Q_TILE = 1024
KEY_LOOP_UNROLL = 4
LOG2_E = 1.4426950408889634
FF_CHUNK = 256
VMEM_LIMIT = 56 * 1024 * 1024

_NT = (((1,), (1,)), ((), ()))


def _rms(x, g):
    return x * lax.rsqrt(jnp.mean(x * x, axis=-1, keepdims=True) + EPS) * g


def _sigmoid(x):
    return 1.0 / (1.0 + jnp.exp(-x))


def _gelu_tanh(x):
    return x * (0.5 * (1.0 + jnp.tanh(0.7978845608028654 * (x + 0.044715 * (x * x * x)))))


def _select_rows(e, x):
    hi = x.astype(BF16)
    lo = (x - hi.astype(F32)).astype(BF16)
    return (jnp.dot(e, hi, preferred_element_type=F32)
            + jnp.dot(e, lo, preferred_element_type=F32))


def _causal_conv3(u, scr, w, prev, hist):
    tm = u.shape[0]
    scr[0:SUBLANES, :] = jnp.zeros((SUBLANES, u.shape[1]), F32) if prev is None else prev
    scr[SUBLANES:tm + SUBLANES, :] = u
    s1 = scr[SUBLANES - 1:tm + SUBLANES - 1, :]
    s2 = scr[SUBLANES - 2:tm + SUBLANES - 2, :]
    if hist is not None:
        h1, h2, tmod = hist
        s1 = jnp.where(tmod == 0, h1, s1)
        s2 = jnp.where(tmod < 2, h2, s2)
    return w[0:1, :] * s2 + w[1:2, :] * s1 + w[2:3, :] * u


def _mixer_in_kernel(*refs, hist_mode):
    if hist_mode:
        (x_ref, g_ref, win_ref, wca_ref, wa_ref, e1_ref, e2_ref, st_ref,
         q_ref, kf_ref, vf_ref, ga_ref, sgb_ref, u_ref, scr) = refs
    else:
        (x_ref, g_ref, win_ref, wca_ref, wa_ref,
         q_ref, k16_ref, v16_ref, kf_ref, vf_ref, ga_ref, sgb_ref, cst_ref, scr, tail_scr) = refs
    tm = x_ref.shape[0]
    h = _rms(x_ref[...], g_ref[...]).astype(BF16)

    def proj(lo, hi):
        return jnp.dot(h, win_ref[:, lo:hi], preferred_element_type=F32)

    cw = SB_WIDTH
    u = proj(cw, 2 * cw) * proj(2 * cw, 3 * cw)
    if hist_mode:
        st = st_ref[...]
        tmod = lax.broadcasted_iota(jnp.int32, u.shape, 0) & (SUBLANES - 1)
        hist = (_select_rows(e1_ref[...], st), _select_rows(e2_ref[...], st), tmod)
        prev = None
        u_ref[...] = u
    else:
        hist = None

        @pl.when(pl.program_id(1) == 0)
        def _():
            tail_scr[...] = jnp.zeros(tail_scr.shape, F32)

        prev = tail_scr[...]
        tail_scr[...] = u[tm - SUBLANES:tm, :]
        cst_ref[...] = u[tm - (KSIZE - 1):tm, :]
    conv = _causal_conv3(u, scr, wca_ref[...], prev, hist)
    ya = jnp.dot((proj(0, cw) * conv).astype(BF16), wa_ref[...], preferred_element_type=F32)

    q = proj(3 * cw, 4 * cw) * (HEAD_DIM ** -0.5)
    q_ref[...] = q.astype(q_ref.dtype)
    k = proj(4 * cw, 5 * cw)
    kf_ref[...] = k
    v = proj(5 * cw, 6 * cw)
    vf_ref[...] = v
    if not hist_mode:
        k16_ref[...] = k.astype(BF16)
        v16_ref[...] = v.astype(BF16)
    d = ga_ref.shape[1]
    ga_ref[...] = _sigmoid(proj(6 * cw, 6 * cw + d)) * ya
    sgb_ref[...] = _sigmoid(proj(6 * cw + d, 6 * cw + 2 * d))


def _mixer_in(x, g, w_in, w_conv_a, w_a, hist):
    nb, t, d = x.shape
    tm = min(ROW_TILE, t)
    nt = t // tm
    cw = SB_WIDTH
    hist_mode = hist is not None

    def row_spec(c):
        return pl.BlockSpec((None, tm, c), lambda b, i: (b, i, 0))

    def const_spec(shape):
        return pl.BlockSpec(shape, lambda b, i: (0,) * len(shape), pipeline_mode=pl.Buffered(1))

    in_specs = [row_spec(d), const_spec((1, d)), const_spec(w_in.shape),
                const_spec(w_conv_a.shape), const_spec(w_a.shape)]
    args = [x, g, w_in, w_conv_a, w_a]
    rows = lambda c, dt: jax.ShapeDtypeStruct((nb, t, c), dt)
    if hist_mode:
        e1, e2, st = hist
        in_specs += [const_spec(e1.shape), const_spec(e2.shape),
                     pl.BlockSpec((None, tm // 4, cw), lambda b, i: (b, i, 0))]
        args += [e1, e2, st]
        out_shape = [rows(cw, F32), rows(cw, F32), rows(cw, F32), rows(d, F32), rows(d, F32),
                     rows(cw, F32)]
        out_specs = [row_spec(cw)] * 3 + [row_spec(d)] * 2 + [row_spec(cw)]
    else:
        out_shape = [rows(cw, BF16), rows(cw, BF16), rows(cw, BF16), rows(cw, F32), rows(cw, F32),
                     rows(d, F32), rows(d, F32),
                     jax.ShapeDtypeStruct((nb, KSIZE - 1, cw), F32)]
        out_specs = ([row_spec(cw)] * 5 + [row_spec(d)] * 2
                     + [pl.BlockSpec((None, KSIZE - 1, cw), lambda b, i: (b, 0, 0))])
    return pl.pallas_call(
        functools.partial(_mixer_in_kernel, hist_mode=hist_mode),
        grid=(nb, nt),
        in_specs=in_specs,
        out_specs=out_specs,
        out_shape=out_shape,
        scratch_shapes=[pltpu.VMEM((tm + SUBLANES, cw), F32)]
        + ([] if hist_mode else [pltpu.VMEM((SUBLANES, cw), F32)]),
        compiler_params=pltpu.CompilerParams(
            dimension_semantics=("arbitrary", "arbitrary"), vmem_limit_bytes=VMEM_LIMIT),
        name="mixer_in_hist" if hist_mode else "mixer_in",
    )(*args)


def _attn_prompt_kernel(bias_ref, q_ref, k_ref, vt_ref, o_ref, z_ref, a_ref, acc_ref):
    tq = q_ref.shape[0]
    nchunk = tq // LANES
    nwide = tq // MXU_COLS
    slabs = KEY_BLOCK // SUBLANES
    nd = tq // KEY_BLOCK
    hp = pl.program_id(1)
    qt = pl.program_id(2)
    q2 = q_ref[...].astype(F32)
    lane = lax.broadcasted_iota(jnp.int32, (tq, LANES), 1)
    qm = [jnp.where(lane < HEAD_DIM, q2, 0.0).astype(BF16),
          jnp.where(lane >= HEAD_DIM, q2, 0.0).astype(BF16)]
    bias = [bias_ref[2 * hp], bias_ref[2 * hp + 1]]
    qpos = (lax.broadcasted_iota(jnp.int32, (nchunk, LANES), 0) * LANES
            + lax.broadcasted_iota(jnp.int32, (nchunk, LANES), 1))
    acc_ref[...] = jnp.zeros(acc_ref.shape, F32)

    def block(j, p, masked):
        start = pl.multiple_of(j * KEY_BLOCK, KEY_BLOCK)
        kb = k_ref[pl.ds(start, KEY_BLOCK), :]
        for h in range(2):
            for w in range(nwide):
                z = lax.dot_general(kb, qm[h][w * MXU_COLS:(w + 1) * MXU_COLS, :], _NT,
                                    preferred_element_type=F32) + bias[h]
                for half in range(MXU_COLS // LANES):
                    c = w * (MXU_COLS // LANES) + half
                    z_ref[h, :, c * SUBLANES:(c + 1) * SUBLANES, :] = (
                        z[:, half * LANES:(half + 1) * LANES].reshape(slabs, SUBLANES, LANES))
        krel0 = (j - qt * nd) * KEY_BLOCK

        def keys(i, p):
            kbi = slabs - 1 - i
            p = list(p)
            for ks in reversed(range(SUBLANES)):
                for h in range(2):
                    idx = (h, kbi, pl.ds(ks, nchunk, stride=SUBLANES), slice(None))
                    beta = 1.0 / (1.0 + jnp.exp2(z_ref[idx] * -LOG2_E))
                    keep = 1.0 - beta
                    if masked:
                        m = qpos > krel0 + kbi * SUBLANES + ks
                        beta = jnp.where(m, beta, 0.0)
                        keep = jnp.where(m, keep, 1.0)
                    a_ref[idx] = beta * p[h]
                    p[h] = p[h] * keep
            return tuple(p)

        p = lax.fori_loop(0, slabs, keys, p, unroll=KEY_LOOP_UNROLL)
        vt = vt_ref[j]
        for h in range(2):
            for w in range(nwide):
                a = jnp.concatenate(
                    [a_ref[h, :, c * SUBLANES:(c + 1) * SUBLANES, :].reshape(KEY_BLOCK, LANES)
                     for c in range(w * (MXU_COLS // LANES), (w + 1) * (MXU_COLS // LANES))],
                    axis=1).astype(BF16)
                acc_ref[h * HEAD_DIM:(h + 1) * HEAD_DIM, w * MXU_COLS:(w + 1) * MXU_COLS] += jnp.dot(
                    vt[h * HEAD_DIM:(h + 1) * HEAD_DIM, :], a, preferred_element_type=F32)
        return p

    ones = jnp.ones((nchunk, LANES), F32)
    p = lax.fori_loop(0, nd, lambda jj, p: block((qt + 1) * nd - 1 - jj, p, True), (ones, ones))
    lax.fori_loop(0, qt * nd, lambda jj, p: block(qt * nd - 1 - jj, p, False), p)
    o_ref[...] = acc_ref[...].T.astype(o_ref.dtype)


def _attn_prompt(q, k, v, sb_bias):
    nb, t, _ = q.shape
    tq = min(Q_TILE, t)
    npair = SB_WIDTH // LANES
    nkb = t // KEY_BLOCK
    vt = v.reshape(nb, nkb, KEY_BLOCK, npair, LANES).transpose(0, 3, 1, 4, 2)
    return pl.pallas_call(
        _attn_prompt_kernel,
        grid=(nb, npair, t // tq),
        in_specs=[
            pl.BlockSpec(memory_space=pltpu.SMEM),
            pl.BlockSpec((None, tq, LANES), lambda b, h, i: (b, i, h)),
            pl.BlockSpec((None, t, LANES), lambda b, h, i: (b, 0, h)),
            pl.BlockSpec((None, None, nkb, LANES, KEY_BLOCK), lambda b, h, i: (b, h, 0, 0, 0)),
        ],
        out_specs=pl.BlockSpec((None, tq, LANES), lambda b, h, i: (b, i, h)),
        out_shape=jax.ShapeDtypeStruct((nb, t, SB_WIDTH), BF16),
        scratch_shapes=[
            pltpu.VMEM((2, KEY_BLOCK // SUBLANES, (tq // LANES) * SUBLANES, LANES), F32),
            pltpu.VMEM((2, KEY_BLOCK // SUBLANES, (tq // LANES) * SUBLANES, LANES), F32),
            pltpu.VMEM((LANES, tq), F32),
        ],
        compiler_params=pltpu.CompilerParams(
            dimension_semantics=("arbitrary", "arbitrary", "arbitrary"),
            vmem_limit_bytes=VMEM_LIMIT),
        name="attn_prompt",
    )(sb_bias, q, k, vt)


def _attn_sample_kernel(pt_ref, q_ref, kn_ref, vn_ref, bias_ref, bd_ref, w2_ref, *rest, n_pages):
    del pt_ref
    k_pages = rest[:n_pages]
    v_pages = rest[n_pages:2 * n_pages]
    o_ref = rest[2 * n_pages]
    t, width = q_ref.shape
    rows = HEADS * t
    page = k_pages[0].shape[1]
    bd = bd_ref[...]
    qbd = (jnp.concatenate([q_ref[...]] * HEADS, axis=0) * bd).astype(BF16)
    bias = jnp.concatenate([bias_ref[...]] * (KEY_BLOCK // LANES), axis=1)
    w2 = w2_ref[...]
    pad = jnp.zeros((KEY_BLOCK - t, width), F32)
    row = lax.broadcasted_iota(jnp.int32, (rows, KEY_BLOCK), 0)
    col = lax.broadcasted_iota(jnp.int32, (rows, KEY_BLOCK), 1)
    new_mask = col < (row & (t - 1))

    per_block = KEY_BLOCK // page
    past = [list(range(j * per_block, (j + 1) * per_block))
            for j in reversed(range(n_pages // per_block))]
    kn = jnp.concatenate([kn_ref[...], pad], axis=0).astype(BF16)
    zs = [lax.dot_general(qbd, kn, _NT, preferred_element_type=F32) + bias]
    for pages in past:
        zs.append(jnp.concatenate(
            [jnp.dot(qbd, k_pages[g][...].astype(BF16), preferred_element_type=F32)
             for g in pages], axis=1) + bias)
    z = jnp.concatenate(zs, axis=0)
    nblk = len(zs)
    sp = jnp.maximum(z, 0.0) + jnp.log(1.0 + jnp.exp(-jnp.abs(z)))
    spm = jnp.concatenate([jnp.where(new_mask, sp[0:rows], 0.0), sp[rows:]], axis=0)
    hi = spm.astype(BF16)
    lo = (spm - hi.astype(F32)).astype(BF16)
    s = jnp.dot(jnp.concatenate([hi, lo], axis=1), w2, preferred_element_type=F32)
    later = s[:, :KEY_BLOCK]
    total = s[:, KEY_BLOCK:]
    carries = [jnp.zeros((rows, LANES), F32)]
    for b in range(1, nblk):
        carries.append(carries[-1] + total[(b - 1) * rows:b * rows])
    carry = jnp.concatenate(carries, axis=0)
    a = jnp.exp(z - sp - later - jnp.concatenate([carry] * (KEY_BLOCK // LANES), axis=1))
    a = jnp.concatenate([jnp.where(new_mask, a[0:rows], 0.0), a[rows:]], axis=0).astype(BF16)
    vn = jnp.concatenate([vn_ref[...], pad], axis=0).astype(BF16)
    acc = jnp.dot(a[0:rows], vn, preferred_element_type=F32)
    for b, pages in enumerate(past, start=1):
        for r, g in enumerate(pages):
            acc += lax.dot_general(a[b * rows:(b + 1) * rows, r * page:(r + 1) * page],
                                   v_pages[g][...].astype(BF16), _NT,
                                   preferred_element_type=F32)
    sel = acc * bd
    out = sel[0:t, :]
    for hd in range(1, HEADS):
        out = out + sel[hd * t:(hd + 1) * t, :]
    o_ref[...] = out


def _attn_sample(q, k_new, v_new, cache_k, cache_v, page_table, bias_rows, bd, w2):
    nb, t, width = q.shape
    n_pages = page_table.shape[1]
    page = cache_k.shape[2]
    assert KEY_BLOCK % page == 0 and n_pages % (KEY_BLOCK // page) == 0
    tok = pl.BlockSpec((None, t, width), lambda b, pt: (b, 0, 0))

    def const(shape):
        return pl.BlockSpec(shape, lambda b, pt: (0,) * len(shape), pipeline_mode=pl.Buffered(1))

    def page_spec(j):
        return pl.BlockSpec((None, width, page), lambda b, pt, j=j: (pt[b * n_pages + j], 0, 0))

    grid_spec = pltpu.PrefetchScalarGridSpec(
        num_scalar_prefetch=1,
        grid=(nb,),
        in_specs=[tok, tok, tok, const(bias_rows.shape), const(bd.shape), const(w2.shape)]
        + [page_spec(j) for j in range(n_pages)] * 2,
        out_specs=tok,
    )
    return pl.pallas_call(
        functools.partial(_attn_sample_kernel, n_pages=n_pages),
        grid_spec=grid_spec,
        out_shape=jax.ShapeDtypeStruct((nb, t, width), F32),
        compiler_params=pltpu.CompilerParams(
            dimension_semantics=("arbitrary",), vmem_limit_bytes=VMEM_LIMIT),
        name="attn_sample",
    )(page_table.reshape(-1), q, k_new, v_new, bias_rows, bd, w2,
      *([cache_k] * n_pages), *([cache_v] * n_pages))


def _mixer_out_kernel(*refs, hist_mode):
    if hist_mode:
        (x_ref, ga_ref, sgb_ref, o_ref, p_ref, wb_ref, wo_ref, gpm_ref, gpf_ref, wup_ref,
         wcf_ref, bcf_ref, wdn_ref, gqf_ref, wple_ref, gpg_ref, wpg_ref, gpp_ref,
         e1_ref, e2_ref, st_ref, y_ref, up_ref, dn_scr, c_scr) = refs
        tail_scr = None
    else:
        (x_ref, ga_ref, sgb_ref, o_ref, p_ref, wb_ref, wo_ref, gpm_ref, gpf_ref, wup_ref,
         wcf_ref, bcf_ref, wdn_ref, gqf_ref, wple_ref, gpg_ref, wpg_ref, gpp_ref,
         y_ref, fst_ref, dn_scr, c_scr, tail_scr) = refs
    tm = x_ref.shape[0]
    dff = wdn_ref.shape[0]
    yb = jnp.dot(o_ref[...].astype(BF16), wb_ref[...], preferred_element_type=F32)
    merged = ga_ref[...] + sgb_ref[...] * yb
    x1 = x_ref[...] + _rms(jnp.dot(merged.astype(BF16), wo_ref[...],
                                   preferred_element_type=F32), gpm_ref[...])
    h2 = _rms(x1, gpf_ref[...]).astype(BF16)

    if hist_mode:
        tmod = lax.broadcasted_iota(jnp.int32, (tm, FF_CHUNK), 0) & (SUBLANES - 1)
        e1 = e1_ref[...]
        e2 = e2_ref[...]
    else:
        @pl.when(pl.program_id(1) == 0)
        def _():
            tail_scr[...] = jnp.zeros(tail_scr.shape, F32)

    def conv_part(lo):
        hi = lo + FF_CHUNK
        up = jnp.dot(h2, wup_ref[:, lo:hi], preferred_element_type=F32)
        if hist_mode:
            st = st_ref[:, lo:hi]
            hist = (_select_rows(e1, st), _select_rows(e2, st), tmod)
            prev = None
            up_ref[:, lo:hi] = up
        else:
            hist = None
            prev = tail_scr[:, lo:hi]
            tail_scr[:, lo:hi] = up[tm - SUBLANES:tm, :]
            fst_ref[:, lo:hi] = up[tm - (KSIZE - 1):tm, :]
        return _causal_conv3(up, c_scr, wcf_ref[:, lo:hi], prev, hist) + bcf_ref[:, lo:hi]

    for c in range(dff // FF_CHUNK):
        gate = conv_part(c * FF_CHUNK)
        val = conv_part(dff + c * FF_CHUNK)
        part = jnp.dot((_gelu_tanh(gate) * val).astype(BF16),
                       wdn_ref[c * FF_CHUNK:(c + 1) * FF_CHUNK, :], preferred_element_type=F32)
        if c == 0:
            dn_scr[...] = part
        else:
            dn_scr[...] += part
    x2 = x1 + _rms(dn_scr[...], gqf_ref[...])
    gate = _sigmoid(jnp.dot(_rms(x2, gpg_ref[...]).astype(BF16), wpg_ref[...],
                            preferred_element_type=F32))
    e = jnp.dot(p_ref[...].astype(BF16), wple_ref[...], preferred_element_type=F32) * gate
    y_ref[...] = x2 + _rms(e, gpp_ref[...])


def _mixer_out(x, ga, sgb, o, p, wb, wo, gpm, gpf, wup, wcf, bcf, wdn, gqf, wple, gpg, wpg, gpp,
               hist):
    nb, t, d = x.shape
    hist_mode = hist is not None
    tm = min(HIST_OUT_ROW_TILE if hist_mode else ROW_TILE, t)
    nt = t // tm
    dff2 = wup.shape[1]

    def row_spec(c):
        return pl.BlockSpec((None, tm, c), lambda b, i: (b, i, 0))

    def const_spec(shape):
        return pl.BlockSpec(shape, lambda b, i: (0,) * len(shape), pipeline_mode=pl.Buffered(1))

    consts = [wb, wo, gpm, gpf, wup, wcf, bcf, wdn, gqf, wple, gpg, wpg, gpp]
    in_specs = ([row_spec(d), row_spec(d), row_spec(d), row_spec(o.shape[2]), row_spec(p.shape[2])]
                + [const_spec(c.shape) for c in consts])
    args = [x, ga, sgb, o, p] + consts
    scratch = [pltpu.VMEM((tm, d), F32), pltpu.VMEM((tm + SUBLANES, FF_CHUNK), F32)]
    if hist_mode:
        e1, e2, st = hist
        in_specs += [const_spec(e1.shape), const_spec(e2.shape),
                     pl.BlockSpec((None, tm // 4, dff2), lambda b, i: (b, i, 0))]
        args += [e1, e2, st]
        out_shape = [jax.ShapeDtypeStruct((nb, t, d), F32),
                     jax.ShapeDtypeStruct((nb, t, dff2), F32)]
        out_specs = [row_spec(d), row_spec(dff2)]
    else:
        out_shape = [jax.ShapeDtypeStruct((nb, t, d), F32),
                     jax.ShapeDtypeStruct((nb, KSIZE - 1, dff2), F32)]
        out_specs = [row_spec(d), pl.BlockSpec((None, KSIZE - 1, dff2), lambda b, i: (b, 0, 0))]
        scratch.append(pltpu.VMEM((SUBLANES, dff2), F32))
    return pl.pallas_call(
        functools.partial(_mixer_out_kernel, hist_mode=hist_mode),
        grid=(nb, nt),
        in_specs=in_specs,
        out_specs=out_specs,
        out_shape=out_shape,
        scratch_shapes=scratch,
        compiler_params=pltpu.CompilerParams(
            dimension_semantics=("arbitrary", "arbitrary"), vmem_limit_bytes=VMEM_LIMIT),
        name="mixer_out_hist" if hist_mode else "mixer_out",
    )(*args)


def _suffix_sum_matrix():
    j = lax.broadcasted_iota(jnp.int32, (KEY_BLOCK, KEY_BLOCK + LANES), 0)
    s = lax.broadcasted_iota(jnp.int32, (KEY_BLOCK, KEY_BLOCK + LANES), 1)
    w = jnp.where((j > s) | (s >= KEY_BLOCK), 1.0, 0.0).astype(BF16)
    return jnp.concatenate([w, w], axis=0)


def _history_selectors(tm, t):
    r = lax.broadcasted_iota(jnp.int32, (tm, tm // 4), 0)
    c = lax.broadcasted_iota(jnp.int32, (tm, tm // 4), 1)
    seq, pos = r // t, r % t
    e1 = ((pos == 0) & (c == 2 * seq + 1)).astype(BF16)
    e2 = (((pos == 0) & (c == 2 * seq)) | ((pos == 1) & (c == 2 * seq + 1))).astype(BF16)
    return e1, e2


def kernel(x_prompt, x_sample, cache_k, cache_v, state_conv, state_ffn_conv, page_table, p_prompt, p_sample, g_pre_mix, w_in, w_conv_a, w_a, w_b, sb_bias, w_o, g_post_mix, g_pre_ffn, w_up, w_conv_f, b_conv_f, w_down, g_post_ffn, w_ple, g_ple_gate, w_ple_gate, g_post_ple):
    depth = w_in.shape[0]
    nbp, seq, d = x_prompt.shape
    nbs, dseq, _ = x_sample.shape
    assert dseq == SUBLANES and KSIZE == 3
    n_pool, page = cache_k.shape[1], cache_k.shape[2]
    pages_t = lambda c: c.transpose(0, 2, 3, 1).reshape(n_pool, -1, page)
    rows_s = nbs * dseq
    w2 = _suffix_sum_matrix()
    sel_in = _history_selectors(min(ROW_TILE, rows_s), dseq)
    sel_out = _history_selectors(min(HIST_OUT_ROW_TILE, rows_s), dseq)
    r64 = lax.broadcasted_iota(jnp.int32, (HEADS * dseq, SB_WIDTH), 0) // dseq
    c64 = lax.broadcasted_iota(jnp.int32, (HEADS * dseq, SB_WIDTH), 1) // HEAD_DIM
    bd = (r64 == c64).astype(F32)
    row2 = lambda a: a.reshape(1, -1)

    xp = x_prompt
    xs = x_sample.reshape(1, rows_s, d)
    outs = [[] for _ in range(8)]
    for i in range(depth):
        win, wa, wb, wo = (w.astype(BF16) for w in (w_in[i], w_a[i], w_b[i], w_o[i]))
        wup, wdn, wple, wpg = (w.astype(BF16) for w in (w_up[i], w_down[i], w_ple[i], w_ple_gate[i]))
        tail = (wb, wo, row2(g_post_mix[i]), row2(g_pre_ffn[i]), wup, w_conv_f[i],
                row2(b_conv_f[i]), wdn, row2(g_post_ffn[i]), wple, row2(g_ple_gate[i]), wpg,
                row2(g_post_ple[i]))
        q, k16, v16, kf, vf, ga, sgb, cst = _mixer_in(
            xp, row2(g_pre_mix[i]), win, w_conv_a[i], wa, None)
        o = _attn_prompt(q, k16, v16, sb_bias[i])
        xp, fst = _mixer_out(xp, ga, sgb, o, p_prompt[i], *tail, None)
        outs[0].append(kf.reshape(nbp, seq, HEADS, HEAD_DIM))
        outs[1].append(vf.reshape(nbp, seq, HEADS, HEAD_DIM))
        outs[2].append(cst)
        outs[3].append(fst)
        st_a = state_conv[i].reshape(1, nbs * (KSIZE - 1), -1)
        st_f = state_ffn_conv[i].reshape(1, nbs * (KSIZE - 1), -1)
        q, kf, vf, ga, sgb, u = _mixer_in(
            xs, row2(g_pre_mix[i]), win, w_conv_a[i], wa, (*sel_in, st_a))
        bias_rows = jnp.broadcast_to(
            jnp.repeat(sb_bias[i].astype(F32), dseq)[:, None], (HEADS * dseq, LANES))
        o = _attn_sample(q.reshape(nbs, dseq, -1), kf.reshape(nbs, dseq, -1),
                         vf.reshape(nbs, dseq, -1),
                         pages_t(cache_k[i]), pages_t(cache_v[i]),
                         page_table, bias_rows, bd, w2)
        xs, up = _mixer_out(xs, ga, sgb, o.reshape(1, rows_s, -1),
                            p_sample[i].reshape(1, rows_s, -1), *tail, (*sel_out, st_f))
        outs[4].append(kf.reshape(nbs, dseq, HEADS, HEAD_DIM))
        outs[5].append(vf.reshape(nbs, dseq, HEADS, HEAD_DIM))
        outs[6].append(u.reshape(nbs, dseq, -1)[:, dseq - (KSIZE - 1):])
        outs[7].append(up.reshape(nbs, dseq, -1)[:, dseq - (KSIZE - 1):])
    stacked = [jnp.stack(o) for o in outs]
    return (xp, xs.reshape(nbs, dseq, d), *stacked)
```

```python
import functools

import jax
import jax.numpy as jnp
from jax import lax
from jax.experimental import pallas as pl
from jax.experimental.pallas import tpu as pltpu

F32 = jnp.float32
BF16 = jnp.bfloat16

EPS = 1e-6
HEADS = 8
HEAD_DIM = 64
SB_WIDTH = HEADS * HEAD_DIM
KSIZE = 3

LANES = 128
SUBLANES = 8
ROW_TILE = 512
OUT_ROW_TILE = 512
HIST_OUT_ROW_TILE = 256
MXU_COLS = 256
KEY_BLOCK = 256
Q_TILE = 1024
KEY_LOOP_UNROLL = 32
CHUNK_PITCH = 12
FF_CHUNK = 256
VMEM_LIMIT = 56 * 1024 * 1024

_NT = (((1,), (1,)), ((), ()))


def _rms(x, g):
    return x * lax.rsqrt(jnp.mean(x * x, axis=-1, keepdims=True) + EPS) * g


def _sigmoid(x):
    return 1.0 / (1.0 + jnp.exp(-x))


def _gelu_tanh(x):
    return x * (0.5 * (1.0 + jnp.tanh(0.7978845608028654 * (x + 0.044715 * (x * x * x)))))


def _select_rows(e, x):
    hi = x.astype(BF16)
    lo = (x - hi.astype(F32)).astype(BF16)
    return (jnp.dot(e, hi, preferred_element_type=F32)
            + jnp.dot(e, lo, preferred_element_type=F32))


def _causal_conv3(u, scr, w, prev, hist):
    tm = u.shape[0]
    scr[0:SUBLANES, :] = jnp.zeros((SUBLANES, u.shape[1]), F32) if prev is None else prev
    scr[SUBLANES:tm + SUBLANES, :] = u
    s1 = scr[SUBLANES - 1:tm + SUBLANES - 1, :]
    s2 = scr[SUBLANES - 2:tm + SUBLANES - 2, :]
    if hist is not None:
        h1, h2, tmod = hist
        s1 = jnp.where(tmod == 0, h1, s1)
        s2 = jnp.where(tmod < 2, h2, s2)
    return w[0:1, :] * s2 + w[1:2, :] * s1 + w[2:3, :] * u


def _mixer_in_kernel(*refs, hist_mode):
    if hist_mode:
        (x_ref, g_ref, win_ref, wca_ref, wa_ref, e1_ref, e2_ref, st_ref,
         q_ref, kf_ref, vf_ref, ga_ref, sgb_ref, u_ref, scr) = refs
    else:
        (x_ref, g_ref, win_ref, wca_ref, wa_ref,
         q_ref, k16_ref, v16_ref, kf_ref, vf_ref, ga_ref, sgb_ref, cst_ref, scr, tail_scr) = refs
    tm = x_ref.shape[0]
    h = _rms(x_ref[...], g_ref[...]).astype(BF16)

    def proj(lo, hi):
        return jnp.dot(h, win_ref[:, lo:hi], preferred_element_type=F32)

    cw = SB_WIDTH
    u = proj(cw, 2 * cw) * proj(2 * cw, 3 * cw)
    if hist_mode:
        st = st_ref[...]
        tmod = lax.broadcasted_iota(jnp.int32, u.shape, 0) & (SUBLANES - 1)
        hist = (_select_rows(e1_ref[...], st), _select_rows(e2_ref[...], st), tmod)
        prev = None
        u_ref[...] = u
    else:
        hist = None

        @pl.when(pl.program_id(1) == 0)
        def _():
            tail_scr[...] = jnp.zeros(tail_scr.shape, F32)

        prev = tail_scr[...]
        tail_scr[...] = u[tm - SUBLANES:tm, :]
        cst_ref[...] = u[tm - (KSIZE - 1):tm, :]
    conv = _causal_conv3(u, scr, wca_ref[...], prev, hist)
    ya = jnp.dot((proj(0, cw) * conv).astype(BF16), wa_ref[...], preferred_element_type=F32)

    q = proj(3 * cw, 4 * cw) * (HEAD_DIM ** -0.5)
    q_ref[...] = q.astype(q_ref.dtype)
    k = proj(4 * cw, 5 * cw)
    kf_ref[...] = k
    v = proj(5 * cw, 6 * cw)
    vf_ref[...] = v
    if not hist_mode:
        k16_ref[...] = k.astype(BF16)
        v16_ref[...] = v.astype(BF16)
    d = ga_ref.shape[1]
    ga_ref[...] = _sigmoid(proj(6 * cw, 6 * cw + d)) * ya
    sgb_ref[...] = _sigmoid(proj(6 * cw + d, 6 * cw + 2 * d))


def _mixer_in(x, g, w_in, w_conv_a, w_a, hist):
    nb, t, d = x.shape
    tm = min(ROW_TILE, t)
    nt = t // tm
    cw = SB_WIDTH
    hist_mode = hist is not None

    def row_spec(c):
        return pl.BlockSpec((None, tm, c), lambda b, i: (b, i, 0))

    def const_spec(shape):
        return pl.BlockSpec(shape, lambda b, i: (0,) * len(shape), pipeline_mode=pl.Buffered(1))

    in_specs = [row_spec(d), const_spec((1, d)), const_spec(w_in.shape),
                const_spec(w_conv_a.shape), const_spec(w_a.shape)]
    args = [x, g, w_in, w_conv_a, w_a]
    rows = lambda c, dt: jax.ShapeDtypeStruct((nb, t, c), dt)
    if hist_mode:
        e1, e2, st = hist
        in_specs += [const_spec(e1.shape), const_spec(e2.shape),
                     pl.BlockSpec((None, tm // 4, cw), lambda b, i: (b, i, 0))]
        args += [e1, e2, st]
        out_shape = [rows(cw, F32), rows(cw, F32), rows(cw, F32), rows(d, F32), rows(d, F32),
                     rows(cw, F32)]
        out_specs = [row_spec(cw)] * 3 + [row_spec(d)] * 2 + [row_spec(cw)]
    else:
        out_shape = [rows(cw, BF16), rows(cw, BF16), rows(cw, BF16), rows(cw, F32), rows(cw, F32),
                     rows(d, F32), rows(d, F32),
                     jax.ShapeDtypeStruct((nb, KSIZE - 1, cw), F32)]
        out_specs = ([row_spec(cw)] * 5 + [row_spec(d)] * 2
                     + [pl.BlockSpec((None, KSIZE - 1, cw), lambda b, i: (b, 0, 0))])
    return pl.pallas_call(
        functools.partial(_mixer_in_kernel, hist_mode=hist_mode),
        grid=(nb, nt),
        in_specs=in_specs,
        out_specs=out_specs,
        out_shape=out_shape,
        scratch_shapes=[pltpu.VMEM((tm + SUBLANES, cw), F32)]
        + ([] if hist_mode else [pltpu.VMEM((SUBLANES, cw), F32)]),
        compiler_params=pltpu.CompilerParams(
            dimension_semantics=("arbitrary", "arbitrary"), vmem_limit_bytes=VMEM_LIMIT),
        name="mixer_in_hist" if hist_mode else "mixer_in",
    )(*args)


def _attn_prompt_kernel(bias_ref, q_ref, k_ref, vt_ref, o_ref,
                        z0_ref, z1_ref, a0_ref, a1_ref, acc_ref):
    tq = q_ref.shape[0]
    nchunk = tq // LANES
    nwide = tq // MXU_COLS
    slabs = KEY_BLOCK // SUBLANES
    nd = tq // KEY_BLOCK
    hp = pl.program_id(1)
    qt = pl.program_id(2)
    q2 = q_ref[...].astype(F32) * 0.5
    lane = lax.broadcasted_iota(jnp.int32, (tq, LANES), 1)
    klane = lax.broadcasted_iota(jnp.int32, (KEY_BLOCK, LANES), 1)
    qm, k_own, k_ones = [], [], []
    for h in range(2):
        own = (lane < HEAD_DIM) if h == 0 else (lane >= HEAD_DIM)
        l0 = (1 - h) * HEAD_DIM
        b = jnp.full((tq, LANES), 0.5 * bias_ref[2 * hp + h], F32)
        b_hi = b.astype(BF16).astype(F32)
        extra = jnp.where(lane == l0, b_hi, jnp.where(lane == l0 + 1, b - b_hi, 0.0))
        qm.append(jnp.where(own, q2, extra).astype(BF16))
        kown = (klane < HEAD_DIM) if h == 0 else (klane >= HEAD_DIM)
        k_own.append(jnp.where(kown, 1.0, 0.0).astype(BF16))
        k_ones.append(jnp.where((klane == l0) | (klane == l0 + 1), 1.0, 0.0).astype(BF16))
    qpos = (lax.broadcasted_iota(jnp.int32, (nchunk, LANES), 0) * LANES
            + lax.broadcasted_iota(jnp.int32, (nchunk, LANES), 1))
    acc_ref[...] = jnp.zeros(acc_ref.shape, F32)
    z_bufs = (z0_ref, z1_ref)
    a_bufs = (a0_ref, a1_ref)
    last_block = k_ref.shape[0] // KEY_BLOCK - 1

    def logits(j, z_ref):
        start = pl.multiple_of(jnp.maximum(j, 0) * KEY_BLOCK, KEY_BLOCK)
        kb = k_ref[pl.ds(start, KEY_BLOCK), :]
        for h in range(2):
            kh = kb * k_own[h] + k_ones[h]
            for w in range(nwide):
                z = lax.dot_general(kh, qm[h][w * MXU_COLS:(w + 1) * MXU_COLS, :], _NT,
                                    preferred_element_type=F32)
                for half in range(MXU_COLS // LANES):
                    c = w * (MXU_COLS // LANES) + half
                    z_ref[h, :, c * CHUNK_PITCH:c * CHUNK_PITCH + SUBLANES, :] = (
                        z[:, half * LANES:(half + 1) * LANES].reshape(slabs, SUBLANES, LANES))

    def weights(j, z_ref, a_ref, p, masked):
        krel0 = (j - qt * nd) * KEY_BLOCK

        def slab(i, p):
            kbi = slabs - 1 - i
            p = list(p)

            def beta_keep(h, ks):
                idx = (h, kbi, pl.ds(ks, nchunk, stride=CHUNK_PITCH), slice(None))
                beta = 0.5 * jnp.tanh(z_ref[idx]) + 0.5
                keep = 1.0 - beta
                if masked:
                    m = qpos > krel0 + kbi * SUBLANES + ks
                    beta = jnp.where(m, beta, 0.0)
                    keep = jnp.where(m, keep, 1.0)
                return idx, beta, keep

            for ks in reversed(range(SUBLANES)):
                for h in range(2):
                    idx, beta, keep = beta_keep(h, ks)
                    a_ref[idx] = beta * p[h]
                    p[h] = p[h] * keep
            return tuple(p)

        return lax.fori_loop(0, slabs, slab, p, unroll=KEY_LOOP_UNROLL)

    def values(j, a_ref):
        vt = vt_ref[jnp.minimum(j, last_block)]
        for h in range(2):
            for w in range(nwide):
                a = jnp.concatenate(
                    [a_ref[h, :, c * CHUNK_PITCH:c * CHUNK_PITCH + SUBLANES, :].reshape(KEY_BLOCK, LANES)
                     for c in range(w * (MXU_COLS // LANES), (w + 1) * (MXU_COLS // LANES))],
                    axis=1).astype(BF16)
                acc_ref[h * HEAD_DIM:(h + 1) * HEAD_DIM, w * MXU_COLS:(w + 1) * MXU_COLS] += jnp.dot(
                    vt[h * HEAD_DIM:(h + 1) * HEAD_DIM, :], a, preferred_element_type=F32)

    def block_pair(j, p, masked):
        for jb, par in ((j, 1), (j - 1, 0)):
            values(jb + 1, a_bufs[1 - par])
            logits(jb - 1, z_bufs[1 - par])
            p = weights(jb, z_bufs[par], a_bufs[par], p, masked)
        return p

    nblk = (qt + 1) * nd
    a0_ref[...] = jnp.zeros(a0_ref.shape, F32)
    logits(nblk - 1, z1_ref)
    ones = jnp.ones((nchunk, LANES), F32)
    p = lax.fori_loop(0, nd // 2, lambda i, p: block_pair(nblk - 1 - 2 * i, p, True), (ones, ones))
    lax.fori_loop(0, qt * nd // 2, lambda i, p: block_pair(qt * nd - 1 - 2 * i, p, False), p)
    values(0, a0_ref)
    o_ref[...] = acc_ref[...].T.astype(o_ref.dtype)


def _attn_prompt(q, k, v, sb_bias):
    nb, t, _ = q.shape
    tq = min(Q_TILE, t)
    assert (tq // KEY_BLOCK) % 2 == 0
    npair = SB_WIDTH // LANES
    nkb = t // KEY_BLOCK
    vt = v.reshape(nb, nkb, KEY_BLOCK, npair, LANES).transpose(0, 3, 1, 4, 2)
    return pl.pallas_call(
        _attn_prompt_kernel,
        grid=(nb, npair, t // tq),
        in_specs=[
            pl.BlockSpec(memory_space=pltpu.SMEM),
            pl.BlockSpec((None, tq, LANES), lambda b, h, i: (b, i, h)),
            pl.BlockSpec((None, t, LANES), lambda b, h, i: (b, 0, h)),
            pl.BlockSpec((None, None, nkb, LANES, KEY_BLOCK), lambda b, h, i: (b, h, 0, 0, 0)),
        ],
        out_specs=pl.BlockSpec((None, tq, LANES), lambda b, h, i: (b, i, h)),
        out_shape=jax.ShapeDtypeStruct((nb, t, SB_WIDTH), BF16),
        scratch_shapes=[pltpu.VMEM(
            (2, KEY_BLOCK // SUBLANES, (tq // LANES) * CHUNK_PITCH, LANES), F32)] * 4
        + [pltpu.VMEM((LANES, tq), F32)],
        compiler_params=pltpu.CompilerParams(
            dimension_semantics=("arbitrary", "arbitrary", "arbitrary"),
            vmem_limit_bytes=VMEM_LIMIT),
        name="attn_prompt",
    )(sb_bias, q, k, vt)


def _attn_sample_kernel(pt_ref, q_ref, kn_ref, vn_ref, bias_ref, bd_ref, w2_ref, *rest, n_pages):
    del pt_ref
    k_pages = rest[:n_pages]
    v_pages = rest[n_pages:2 * n_pages]
    o_ref = rest[2 * n_pages]
    t, width = q_ref.shape
    rows = HEADS * t
    page = k_pages[0].shape[1]
    bd = bd_ref[...]
    qbd = (jnp.concatenate([q_ref[...]] * HEADS, axis=0) * bd).astype(BF16)
    bias = jnp.concatenate([bias_ref[...]] * (KEY_BLOCK // LANES), axis=1)
    w2 = w2_ref[...]
    pad = jnp.zeros((KEY_BLOCK - t, width), F32)
    row = lax.broadcasted_iota(jnp.int32, (rows, KEY_BLOCK), 0)
    col = lax.broadcasted_iota(jnp.int32, (rows, KEY_BLOCK), 1)
    new_mask = col < (row & (t - 1))

    per_block = KEY_BLOCK // page
    past = [list(range(j * per_block, (j + 1) * per_block))
            for j in reversed(range(n_pages // per_block))]
    kn = jnp.concatenate([kn_ref[...], pad], axis=0).astype(BF16)
    zs = [lax.dot_general(qbd, kn, _NT, preferred_element_type=F32) + bias]
    for pages in past:
        zs.append(jnp.concatenate(
            [jnp.dot(qbd, k_pages[g][...].astype(BF16), preferred_element_type=F32)
             for g in pages], axis=1) + bias)
    z = jnp.concatenate(zs, axis=0)
    nblk = len(zs)
    sp = jnp.maximum(z, 0.0) + jnp.log(1.0 + jnp.exp(-jnp.abs(z)))
    spm = jnp.concatenate([jnp.where(new_mask, sp[0:rows], 0.0), sp[rows:]], axis=0)
    hi = spm.astype(BF16)
    lo = (spm - hi.astype(F32)).astype(BF16)
    s = jnp.dot(jnp.concatenate([hi, lo], axis=1), w2, preferred_element_type=F32)
    later = s[:, :KEY_BLOCK]
    total = s[:, KEY_BLOCK:]
    carries = [jnp.zeros((rows, LANES), F32)]
    for b in range(1, nblk):
        carries.append(carries[-1] + total[(b - 1) * rows:b * rows])
    carry = jnp.concatenate(carries, axis=0)
    a = jnp.exp(z - sp - later - jnp.concatenate([carry] * (KEY_BLOCK // LANES), axis=1))
    a = jnp.concatenate([jnp.where(new_mask, a[0:rows], 0.0), a[rows:]], axis=0).astype(BF16)
    vn = jnp.concatenate([vn_ref[...], pad], axis=0).astype(BF16)
    acc = jnp.dot(a[0:rows], vn, preferred_element_type=F32)
    for b, pages in enumerate(past, start=1):
        for r, g in enumerate(pages):
            acc += lax.dot_general(a[b * rows:(b + 1) * rows, r * page:(r + 1) * page],
                                   v_pages[g][...].astype(BF16), _NT,
                                   preferred_element_type=F32)
    sel = acc * bd
    out = sel[0:t, :]
    for hd in range(1, HEADS):
        out = out + sel[hd * t:(hd + 1) * t, :]
    o_ref[...] = out


def _attn_sample(q, k_new, v_new, cache_k, cache_v, page_table, bias_rows, bd, w2):
    nb, t, width = q.shape
    n_pages = page_table.shape[1]
    page = cache_k.shape[2]
    assert KEY_BLOCK % page == 0 and n_pages % (KEY_BLOCK // page) == 0
    tok = pl.BlockSpec((None, t, width), lambda b, pt: (b, 0, 0))

    def const(shape):
        return pl.BlockSpec(shape, lambda b, pt: (0,) * len(shape), pipeline_mode=pl.Buffered(1))

    def page_spec(j):
        return pl.BlockSpec((None, width, page), lambda b, pt, j=j: (pt[b * n_pages + j], 0, 0))

    grid_spec = pltpu.PrefetchScalarGridSpec(
        num_scalar_prefetch=1,
        grid=(nb,),
        in_specs=[tok, tok, tok, const(bias_rows.shape), const(bd.shape), const(w2.shape)]
        + [page_spec(j) for j in range(n_pages)] * 2,
        out_specs=tok,
    )
    return pl.pallas_call(
        functools.partial(_attn_sample_kernel, n_pages=n_pages),
        grid_spec=grid_spec,
        out_shape=jax.ShapeDtypeStruct((nb, t, width), F32),
        compiler_params=pltpu.CompilerParams(
            dimension_semantics=("arbitrary",), vmem_limit_bytes=VMEM_LIMIT),
        name="attn_sample",
    )(page_table.reshape(-1), q, k_new, v_new, bias_rows, bd, w2,
      *([cache_k] * n_pages), *([cache_v] * n_pages))


def _mixer_out_kernel(*refs, hist_mode):
    if hist_mode:
        (x_ref, ga_ref, sgb_ref, o_ref, p_ref, wb_ref, wo_ref, gpm_ref, gpf_ref, wup_ref,
         wcf_ref, bcf_ref, wdn_ref, gqf_ref, wple_ref, gpg_ref, wpg_ref, gpp_ref,
         e1_ref, e2_ref, st_ref, y_ref, up_ref, act_scr, c_scr) = refs
        tail_scr = None
    else:
        (x_ref, ga_ref, sgb_ref, o_ref, p_ref, wb_ref, wo_ref, gpm_ref, gpf_ref, wup_ref,
         wcf_ref, bcf_ref, wdn_ref, gqf_ref, wple_ref, gpg_ref, wpg_ref, gpp_ref,
         y_ref, fst_ref, act_scr, c_scr, tail_scr) = refs
    tm = x_ref.shape[0]
    dff = wdn_ref.shape[0]
    yb = jnp.dot(o_ref[...].astype(BF16), wb_ref[...], preferred_element_type=F32)
    merged = ga_ref[...] + sgb_ref[...] * yb
    x1 = x_ref[...] + _rms(jnp.dot(merged.astype(BF16), wo_ref[...],
                                   preferred_element_type=F32), gpm_ref[...])
    h2 = _rms(x1, gpf_ref[...]).astype(BF16)

    if hist_mode:
        tmod = lax.broadcasted_iota(jnp.int32, (tm, FF_CHUNK), 0) & (SUBLANES - 1)
        e1 = e1_ref[...]
        e2 = e2_ref[...]
    else:
        @pl.when(pl.program_id(1) == 0)
        def _():
            tail_scr[...] = jnp.zeros(tail_scr.shape, F32)

    def conv_part(lo, stage):
        hi = lo + FF_CHUNK
        up = jnp.dot(h2, wup_ref[:, lo:hi], preferred_element_type=F32)
        if hist_mode:
            st = st_ref[:, lo:hi]
            hist = (_select_rows(e1, st), _select_rows(e2, st), tmod)
            prev = None
            up_ref[:, lo:hi] = up
        else:
            hist = None
            prev = tail_scr[:, lo:hi]
            tail_scr[:, lo:hi] = up[tm - SUBLANES:tm, :]
            fst_ref[:, lo:hi] = up[tm - (KSIZE - 1):tm, :]
        return _causal_conv3(up, stage, wcf_ref[:, lo:hi], prev, hist) + bcf_ref[:, lo:hi]

    for c in range(dff // FF_CHUNK):
        gate = conv_part(c * FF_CHUNK, c_scr.at[2 * (c % 2)])
        val = conv_part(dff + c * FF_CHUNK, c_scr.at[2 * (c % 2) + 1])
        act_scr[:, c * FF_CHUNK:(c + 1) * FF_CHUNK] = (_gelu_tanh(gate) * val).astype(BF16)
    dn = jnp.dot(act_scr[...], wdn_ref[...], preferred_element_type=F32)
    x2 = x1 + _rms(dn, gqf_ref[...])
    gate = _sigmoid(jnp.dot(_rms(x2, gpg_ref[...]).astype(BF16), wpg_ref[...],
                            preferred_element_type=F32))
    e = jnp.dot(p_ref[...].astype(BF16), wple_ref[...], preferred_element_type=F32) * gate
    y_ref[...] = x2 + _rms(e, gpp_ref[...])


def _mixer_out(x, ga, sgb, o, p, wb, wo, gpm, gpf, wup, wcf, bcf, wdn, gqf, wple, gpg, wpg, gpp,
               hist):
    nb, t, d = x.shape
    hist_mode = hist is not None
    tm = min(HIST_OUT_ROW_TILE if hist_mode else OUT_ROW_TILE, t)
    nt = t // tm
    dff2 = wup.shape[1]

    def row_spec(c):
        return pl.BlockSpec((None, tm, c), lambda b, i: (b, i, 0))

    def const_spec(shape):
        return pl.BlockSpec(shape, lambda b, i: (0,) * len(shape), pipeline_mode=pl.Buffered(1))

    consts = [wb, wo, gpm, gpf, wup, wcf, bcf, wdn, gqf, wple, gpg, wpg, gpp]
    in_specs = ([row_spec(d), row_spec(d), row_spec(d), row_spec(o.shape[2]), row_spec(p.shape[2])]
                + [const_spec(c.shape) for c in consts])
    args = [x, ga, sgb, o, p] + consts
    scratch = [pltpu.VMEM((tm, dff2 // 2), BF16), pltpu.VMEM((4, tm + SUBLANES, FF_CHUNK), F32)]
    if hist_mode:
        e1, e2, st = hist
        in_specs += [const_spec(e1.shape), const_spec(e2.shape),
                     pl.BlockSpec((None, tm // 4, dff2), lambda b, i: (b, i, 0))]
        args += [e1, e2, st]
        out_shape = [jax.ShapeDtypeStruct((nb, t, d), F32),
                     jax.ShapeDtypeStruct((nb, t, dff2), F32)]
        out_specs = [row_spec(d), row_spec(dff2)]
    else:
        out_shape = [jax.ShapeDtypeStruct((nb, t, d), F32),
                     jax.ShapeDtypeStruct((nb, KSIZE - 1, dff2), F32)]
        out_specs = [row_spec(d), pl.BlockSpec((None, KSIZE - 1, dff2), lambda b, i: (b, 0, 0))]
        scratch.append(pltpu.VMEM((SUBLANES, dff2), F32))
    return pl.pallas_call(
        functools.partial(_mixer_out_kernel, hist_mode=hist_mode),
        grid=(nb, nt),
        in_specs=in_specs,
        out_specs=out_specs,
        out_shape=out_shape,
        scratch_shapes=scratch,
        compiler_params=pltpu.CompilerParams(
            dimension_semantics=("arbitrary", "arbitrary"), vmem_limit_bytes=VMEM_LIMIT),
        name="mixer_out_hist" if hist_mode else "mixer_out",
    )(*args)


def _suffix_sum_matrix():
    j = lax.broadcasted_iota(jnp.int32, (KEY_BLOCK, KEY_BLOCK + LANES), 0)
    s = lax.broadcasted_iota(jnp.int32, (KEY_BLOCK, KEY_BLOCK + LANES), 1)
    w = jnp.where((j > s) | (s >= KEY_BLOCK), 1.0, 0.0).astype(BF16)
    return jnp.concatenate([w, w], axis=0)


def _history_selectors(tm, t):
    r = lax.broadcasted_iota(jnp.int32, (tm, tm // 4), 0)
    c = lax.broadcasted_iota(jnp.int32, (tm, tm // 4), 1)
    seq, pos = r // t, r % t
    e1 = ((pos == 0) & (c == 2 * seq + 1)).astype(BF16)
    e2 = (((pos == 0) & (c == 2 * seq)) | ((pos == 1) & (c == 2 * seq + 1))).astype(BF16)
    return e1, e2


def kernel(x_prompt, x_sample, cache_k, cache_v, state_conv, state_ffn_conv, page_table, p_prompt, p_sample, g_pre_mix, w_in, w_conv_a, w_a, w_b, sb_bias, w_o, g_post_mix, g_pre_ffn, w_up, w_conv_f, b_conv_f, w_down, g_post_ffn, w_ple, g_ple_gate, w_ple_gate, g_post_ple):
    depth = w_in.shape[0]
    nbp, seq, d = x_prompt.shape
    nbs, dseq, _ = x_sample.shape
    assert dseq == SUBLANES and KSIZE == 3
    n_pool, page = cache_k.shape[1], cache_k.shape[2]
    pages_t = lambda c: c.transpose(0, 2, 3, 1).reshape(n_pool, -1, page)
    rows_s = nbs * dseq
    w2 = _suffix_sum_matrix()
    sel_in = _history_selectors(min(ROW_TILE, rows_s), dseq)
    sel_out = _history_selectors(min(HIST_OUT_ROW_TILE, rows_s), dseq)
    r64 = lax.broadcasted_iota(jnp.int32, (HEADS * dseq, SB_WIDTH), 0) // dseq
    c64 = lax.broadcasted_iota(jnp.int32, (HEADS * dseq, SB_WIDTH), 1) // HEAD_DIM
    bd = (r64 == c64).astype(F32)
    row2 = lambda a: a.reshape(1, -1)

    xp = x_prompt
    xs = x_sample.reshape(1, rows_s, d)
    outs = [[] for _ in range(8)]
    for i in range(depth):
        win, wa, wb, wo = (w.astype(BF16) for w in (w_in[i], w_a[i], w_b[i], w_o[i]))
        wup, wdn, wple, wpg = (w.astype(BF16) for w in (w_up[i], w_down[i], w_ple[i], w_ple_gate[i]))
        tail = (wb, wo, row2(g_post_mix[i]), row2(g_pre_ffn[i]), wup, w_conv_f[i],
                row2(b_conv_f[i]), wdn, row2(g_post_ffn[i]), wple, row2(g_ple_gate[i]), wpg,
                row2(g_post_ple[i]))
        q, k16, v16, kf, vf, ga, sgb, cst = _mixer_in(
            xp, row2(g_pre_mix[i]), win, w_conv_a[i], wa, None)
        o = _attn_prompt(q, k16, v16, sb_bias[i])
        xp, fst = _mixer_out(xp, ga, sgb, o, p_prompt[i], *tail, None)
        outs[0].append(kf.reshape(nbp, seq, HEADS, HEAD_DIM))
        outs[1].append(vf.reshape(nbp, seq, HEADS, HEAD_DIM))
        outs[2].append(cst)
        outs[3].append(fst)
        st_a = state_conv[i].reshape(1, nbs * (KSIZE - 1), -1)
        st_f = state_ffn_conv[i].reshape(1, nbs * (KSIZE - 1), -1)
        q, kf, vf, ga, sgb, u = _mixer_in(
            xs, row2(g_pre_mix[i]), win, w_conv_a[i], wa, (*sel_in, st_a))
        bias_rows = jnp.broadcast_to(
            jnp.repeat(sb_bias[i].astype(F32), dseq)[:, None], (HEADS * dseq, LANES))
        o = _attn_sample(q.reshape(nbs, dseq, -1), kf.reshape(nbs, dseq, -1),
                         vf.reshape(nbs, dseq, -1),
                         pages_t(cache_k[i]), pages_t(cache_v[i]),
                         page_table, bias_rows, bd, w2)
        xs, up = _mixer_out(xs, ga, sgb, o.reshape(1, rows_s, -1),
                            p_sample[i].reshape(1, rows_s, -1), *tail, (*sel_out, st_f))
        outs[4].append(kf.reshape(nbs, dseq, HEADS, HEAD_DIM))
        outs[5].append(vf.reshape(nbs, dseq, HEADS, HEAD_DIM))
        outs[6].append(u.reshape(nbs, dseq, -1)[:, dseq - (KSIZE - 1):])
        outs[7].append(up.reshape(nbs, dseq, -1)[:, dseq - (KSIZE - 1):])
    stacked = [jnp.stack(o) for o in outs]
    return (xp, xs.reshape(nbs, dseq, d), *stacked)
```

```python
import functools

import jax
import jax.numpy as jnp
from jax import lax
from jax.experimental import pallas as pl
from jax.experimental.pallas import tpu as pltpu

F32 = jnp.float32
BF16 = jnp.bfloat16

EPS = 1e-6
HEADS = 8
HEAD_DIM = 64
SB_WIDTH = HEADS * HEAD_DIM
KSIZE = 3

LANES = 128
SUBLANES = 8
ROW_TILE = 512
OUT_ROW_TILE = 512
HIST_OUT_ROW_TILE = 256
MXU_COLS = 256
KEY_BLOCK = 256
Q_TILE = 1024
KEY_LOOP_UNROLL = 32
BLOCKS_PER_STEP = 2
CHUNK_PITCH = 12
FF_CHUNK = 256
VMEM_LIMIT = 56 * 1024 * 1024

_NT = (((1,), (1,)), ((), ()))


def _rms(x, g):
    return x * lax.rsqrt(jnp.mean(x * x, axis=-1, keepdims=True) + EPS) * g


def _sigmoid(x):
    return 1.0 / (1.0 + jnp.exp(-x))


def _twice_gelu_tanh(x):
    c = 0.7978845608028654
    return x * (1.0 + jnp.tanh(x * (c + (c * 0.044715) * (x * x))))


def _select_rows(e, x):
    hi = x.astype(BF16)
    lo = (x - hi.astype(F32)).astype(BF16)
    return (jnp.dot(e, hi, preferred_element_type=F32)
            + jnp.dot(e, lo, preferred_element_type=F32))


def _history_rows(ea, eb, st0, st1):
    return _select_rows(ea, st1), _select_rows(ea, st0) + _select_rows(eb, st1)


def _causal_conv3(u, scr, w, prev, hist):
    tm = u.shape[0]
    scr[0:SUBLANES, :] = jnp.zeros((SUBLANES, u.shape[1]), F32) if prev is None else prev
    scr[SUBLANES:tm + SUBLANES, :] = u
    s1 = scr[SUBLANES - 1:tm + SUBLANES - 1, :]
    s2 = scr[SUBLANES - 2:tm + SUBLANES - 2, :]
    if hist is not None:
        h1, h2, tmod = hist
        s1 = jnp.where(tmod == 0, h1, s1)
        s2 = jnp.where(tmod < 2, h2, s2)
    return w[0:1, :] * s2 + w[1:2, :] * s1 + w[2:3, :] * u


def _mixer_in_kernel(*refs, hist_mode):
    if hist_mode:
        (x_ref, g_ref, win_ref, wca_ref, wa_ref, ea_ref, eb_ref, st0_ref, st1_ref,
         q_ref, kf_ref, vf_ref, ga_ref, sgb_ref, u_ref, scr) = refs
    else:
        (x_ref, g_ref, win_ref, wca_ref, wa_ref,
         q_ref, k16_ref, vt_ref, kf_ref, vf_ref, ga_ref, sgb_ref, cst_ref, scr, tail_scr) = refs
    tm = x_ref.shape[0]
    h = _rms(x_ref[...], g_ref[...]).astype(BF16)

    def proj(lo, hi):
        return jnp.dot(h, win_ref[:, lo:hi], preferred_element_type=F32)

    cw = SB_WIDTH
    u = proj(cw, 2 * cw) * proj(2 * cw, 3 * cw)
    if hist_mode:
        tmod = lax.broadcasted_iota(jnp.int32, u.shape, 0) & (SUBLANES - 1)
        hist = _history_rows(ea_ref[...], eb_ref[...], st0_ref[...], st1_ref[...]) + (tmod,)
        prev = None
        u_ref[...] = u
    else:
        hist = None

        @pl.when(pl.program_id(1) == 0)
        def _():
            tail_scr[...] = jnp.zeros(tail_scr.shape, F32)

        prev = tail_scr[...]
        tail_scr[...] = u[tm - SUBLANES:tm, :]
        cst_ref[...] = u[tm - (KSIZE - 1):tm, :]
    conv = _causal_conv3(u, scr, wca_ref[...], prev, hist)
    ya = jnp.dot((proj(0, cw) * conv).astype(BF16), wa_ref[...], preferred_element_type=F32)

    q = proj(3 * cw, 4 * cw) * (HEAD_DIM ** -0.5)
    q_ref[...] = q.astype(q_ref.dtype)
    k = proj(4 * cw, 5 * cw)
    kf_ref[...] = k
    v = proj(5 * cw, 6 * cw)
    vf_ref[...] = v
    if not hist_mode:
        k16_ref[...] = k.astype(BF16)
        for hp in range(cw // LANES):
            for r in range(tm // KEY_BLOCK):
                vt_ref[hp, r] = v[r * KEY_BLOCK:(r + 1) * KEY_BLOCK,
                                  hp * LANES:(hp + 1) * LANES].T.astype(BF16)
    d = ga_ref.shape[1]
    ga_ref[...] = _sigmoid(proj(6 * cw, 6 * cw + d)) * ya
    sgb_ref[...] = _sigmoid(proj(6 * cw + d, 6 * cw + 2 * d))


def _mixer_in(x, g, w_in, w_conv_a, w_a, hist):
    nb, t, d = x.shape
    tm = min(ROW_TILE, t)
    nt = t // tm
    cw = SB_WIDTH
    hist_mode = hist is not None
    assert hist_mode or tm % KEY_BLOCK == 0

    def row_spec(c):
        return pl.BlockSpec((None, tm, c), lambda b, i: (b, i, 0))

    def const_spec(shape):
        return pl.BlockSpec(shape, lambda b, i: (0,) * len(shape), pipeline_mode=pl.Buffered(1))

    in_specs = [row_spec(d), const_spec((1, d)), const_spec(w_in.shape),
                const_spec(w_conv_a.shape), const_spec(w_a.shape)]
    args = [x, g, w_in, w_conv_a, w_a]
    rows = lambda c, dt: jax.ShapeDtypeStruct((nb, t, c), dt)
    if hist_mode:
        ea, eb, st0, st1 = hist
        st_spec = pl.BlockSpec((None, tm // SUBLANES, cw), lambda b, i: (b, i, 0))
        in_specs += [const_spec(ea.shape), const_spec(eb.shape), st_spec, st_spec]
        args += [ea, eb, st0, st1]
        out_shape = [rows(cw, F32), rows(cw, F32), rows(cw, F32), rows(d, F32), rows(d, F32),
                     rows(cw, F32)]
        out_specs = [row_spec(cw)] * 3 + [row_spec(d)] * 2 + [row_spec(cw)]
    else:
        npair = cw // LANES
        out_shape = [rows(cw, BF16), rows(cw, BF16),
                     jax.ShapeDtypeStruct((nb, npair, t // KEY_BLOCK, LANES, KEY_BLOCK), BF16),
                     rows(cw, F32), rows(cw, F32), rows(d, F32), rows(d, F32),
                     jax.ShapeDtypeStruct((nb, KSIZE - 1, cw), F32)]
        out_specs = ([row_spec(cw)] * 2
                     + [pl.BlockSpec((None, npair, tm // KEY_BLOCK, LANES, KEY_BLOCK),
                                     lambda b, i: (b, 0, i, 0, 0))]
                     + [row_spec(cw)] * 2 + [row_spec(d)] * 2
                     + [pl.BlockSpec((None, KSIZE - 1, cw), lambda b, i: (b, 0, 0))])
    return pl.pallas_call(
        functools.partial(_mixer_in_kernel, hist_mode=hist_mode),
        grid=(nb, nt),
        in_specs=in_specs,
        out_specs=out_specs,
        out_shape=out_shape,
        scratch_shapes=[pltpu.VMEM((tm + SUBLANES, cw), F32)]
        + ([] if hist_mode else [pltpu.VMEM((SUBLANES, cw), F32)]),
        compiler_params=pltpu.CompilerParams(
            dimension_semantics=("arbitrary", "arbitrary"), vmem_limit_bytes=VMEM_LIMIT),
        name="mixer_in_hist" if hist_mode else "mixer_in",
    )(*args)


def _attn_prompt_kernel(bias_ref, q_ref, k_ref, vt_ref, o_ref,
                        z0_ref, z1_ref, a0_ref, a1_ref, acc_ref):
    tq = q_ref.shape[0]
    nchunk = tq // LANES
    nwide = tq // MXU_COLS
    slabs = KEY_BLOCK // SUBLANES
    nd = tq // KEY_BLOCK
    hp = pl.program_id(1)
    qt = pl.program_id(2)
    q2 = q_ref[...].astype(F32) * 0.5
    lane = lax.broadcasted_iota(jnp.int32, (tq, LANES), 1)
    klane = lax.broadcasted_iota(jnp.int32, (KEY_BLOCK, LANES), 1)
    qm, k_own, k_ones = [], [], []
    for h in range(2):
        own = (lane < HEAD_DIM) if h == 0 else (lane >= HEAD_DIM)
        l0 = (1 - h) * HEAD_DIM
        b = jnp.full((tq, LANES), 0.5 * bias_ref[2 * hp + h], F32)
        b_hi = b.astype(BF16).astype(F32)
        extra = jnp.where(lane == l0, b_hi, jnp.where(lane == l0 + 1, b - b_hi, 0.0))
        qm.append(jnp.where(own, q2, extra).astype(BF16))
        kown = (klane < HEAD_DIM) if h == 0 else (klane >= HEAD_DIM)
        k_own.append(jnp.where(kown, 1.0, 0.0).astype(BF16))
        k_ones.append(jnp.where((klane == l0) | (klane == l0 + 1), 1.0, 0.0).astype(BF16))
    qpos = (lax.broadcasted_iota(jnp.int32, (nchunk, LANES), 0) * LANES
            + lax.broadcasted_iota(jnp.int32, (nchunk, LANES), 1))
    acc_ref[...] = jnp.zeros(acc_ref.shape, F32)
    z_bufs = (z0_ref, z1_ref)
    a_bufs = (a0_ref, a1_ref)
    last_block = k_ref.shape[0] // KEY_BLOCK - 1

    def logits(j, z_ref):
        start = pl.multiple_of(jnp.maximum(j, 0) * KEY_BLOCK, KEY_BLOCK)
        kb = k_ref[pl.ds(start, KEY_BLOCK), :]
        for h in range(2):
            kh = kb * k_own[h] + k_ones[h]
            for w in range(nwide):
                z = lax.dot_general(kh, qm[h][w * MXU_COLS:(w + 1) * MXU_COLS, :], _NT,
                                    preferred_element_type=F32)
                for half in range(MXU_COLS // LANES):
                    c = w * (MXU_COLS // LANES) + half
                    z_ref[h, :, c * CHUNK_PITCH:c * CHUNK_PITCH + SUBLANES, :] = (
                        z[:, half * LANES:(half + 1) * LANES].reshape(slabs, SUBLANES, LANES))

    def weights(j, z_ref, a_ref, p, masked):
        krel0 = (j - qt * nd) * KEY_BLOCK

        def slab(i, p):
            kbi = slabs - 1 - i
            p = list(p)

            def beta_keep(h, ks):
                idx = (h, kbi, pl.ds(ks, nchunk, stride=CHUNK_PITCH), slice(None))
                beta = 0.5 * jnp.tanh(z_ref[idx]) + 0.5
                if masked:
                    beta = jnp.where(qpos > krel0 + kbi * SUBLANES + ks, beta, 0.0)
                return idx, beta, 1.0 - beta

            for ks in reversed(range(SUBLANES)):
                for h in range(2):
                    idx, beta, keep = beta_keep(h, ks)
                    a_ref[idx] = beta * p[h]
                    p[h] = p[h] * keep
            return tuple(p)

        return lax.fori_loop(0, slabs, slab, p, unroll=KEY_LOOP_UNROLL)

    def values(j, a_ref):
        vt = vt_ref[jnp.minimum(j, last_block)]
        for h in range(2):
            for w in range(nwide):
                a = jnp.concatenate(
                    [a_ref[h, :, c * CHUNK_PITCH:c * CHUNK_PITCH + SUBLANES, :].reshape(KEY_BLOCK, LANES)
                     for c in range(w * (MXU_COLS // LANES), (w + 1) * (MXU_COLS // LANES))],
                    axis=1).astype(BF16)
                acc_ref[h * HEAD_DIM:(h + 1) * HEAD_DIM, w * MXU_COLS:(w + 1) * MXU_COLS] += jnp.dot(
                    vt[h * HEAD_DIM:(h + 1) * HEAD_DIM, :], a, preferred_element_type=F32)

    def block_group(j, p, masked):
        for s in range(BLOCKS_PER_STEP):
            jb, par = j - s, 1 - s % 2
            values(jb + 1, a_bufs[1 - par])
            p = weights(jb, z_bufs[par], a_bufs[par], p, masked)
            logits(jb - 1, z_bufs[1 - par])
        return p

    nblk = (qt + 1) * nd
    a0_ref[...] = jnp.zeros(a0_ref.shape, F32)
    logits(nblk - 1, z1_ref)
    ones = jnp.ones((nchunk, LANES), F32)
    step = BLOCKS_PER_STEP
    p = lax.fori_loop(0, nd // step,
                      lambda i, p: block_group(nblk - 1 - step * i, p, True), (ones, ones))
    lax.fori_loop(0, qt * nd // step,
                  lambda i, p: block_group(qt * nd - 1 - step * i, p, False), p)
    values(0, a0_ref)
    o_ref[...] = acc_ref[...].T.astype(o_ref.dtype)


def _attn_prompt(q, k, vt, sb_bias):
    nb, t, _ = q.shape
    tq = min(Q_TILE, t)
    assert (tq // KEY_BLOCK) % BLOCKS_PER_STEP == 0 and BLOCKS_PER_STEP % 2 == 0
    npair = SB_WIDTH // LANES
    nkb = t // KEY_BLOCK
    return pl.pallas_call(
        _attn_prompt_kernel,
        grid=(nb, npair, t // tq),
        in_specs=[
            pl.BlockSpec(memory_space=pltpu.SMEM),
            pl.BlockSpec((None, tq, LANES), lambda b, h, i: (b, i, h)),
            pl.BlockSpec((None, t, LANES), lambda b, h, i: (b, 0, h)),
            pl.BlockSpec((None, None, nkb, LANES, KEY_BLOCK), lambda b, h, i: (b, h, 0, 0, 0)),
        ],
        out_specs=pl.BlockSpec((None, tq, LANES), lambda b, h, i: (b, i, h)),
        out_shape=jax.ShapeDtypeStruct((nb, t, SB_WIDTH), BF16),
        scratch_shapes=[pltpu.VMEM(
            (2, KEY_BLOCK // SUBLANES, (tq // LANES) * CHUNK_PITCH, LANES), F32)] * 4
        + [pltpu.VMEM((LANES, tq), F32)],
        compiler_params=pltpu.CompilerParams(
            dimension_semantics=("arbitrary", "arbitrary", "arbitrary"),
            vmem_limit_bytes=VMEM_LIMIT),
        name="attn_prompt",
    )(sb_bias, q, k, vt)


def _attn_sample_kernel(pt_ref, q_ref, kn_ref, vn_ref, bias_ref, bd_ref, w2_ref, *rest, n_pages):
    del pt_ref
    k_pages = rest[:n_pages]
    v_pages = rest[n_pages:2 * n_pages]
    o_ref = rest[2 * n_pages]
    t, width = q_ref.shape
    rows = HEADS * t
    page = k_pages[0].shape[1]
    bd = bd_ref[...]
    qbd = (jnp.concatenate([q_ref[...]] * HEADS, axis=0) * bd).astype(BF16)
    bias = jnp.concatenate([bias_ref[...]] * (KEY_BLOCK // LANES), axis=1)
    w2 = w2_ref[...]
    pad = jnp.zeros((KEY_BLOCK - t, width), F32)
    row = lax.broadcasted_iota(jnp.int32, (rows, KEY_BLOCK), 0)
    col = lax.broadcasted_iota(jnp.int32, (rows, KEY_BLOCK), 1)
    new_mask = col < (row & (t - 1))

    per_block = KEY_BLOCK // page
    past = [list(range(j * per_block, (j + 1) * per_block))
            for j in reversed(range(n_pages // per_block))]
    kn = jnp.concatenate([kn_ref[...], pad], axis=0).astype(BF16)
    zs = [lax.dot_general(qbd, kn, _NT, preferred_element_type=F32) + bias]
    for pages in past:
        zs.append(jnp.concatenate(
            [jnp.dot(qbd, k_pages[g][...].astype(BF16), preferred_element_type=F32)
             for g in pages], axis=1) + bias)
    z = jnp.concatenate(zs, axis=0)
    nblk = len(zs)
    sp = jnp.maximum(z, 0.0) + jnp.log(1.0 + jnp.exp(-jnp.abs(z)))
    spm = jnp.concatenate([jnp.where(new_mask, sp[0:rows], 0.0), sp[rows:]], axis=0)
    hi = spm.astype(BF16)
    lo = (spm - hi.astype(F32)).astype(BF16)
    s = jnp.dot(jnp.concatenate([hi, lo], axis=1), w2, preferred_element_type=F32)
    later = s[:, :KEY_BLOCK]
    total = s[:, KEY_BLOCK:]
    carries = [jnp.zeros((rows, LANES), F32)]
    for b in range(1, nblk):
        carries.append(carries[-1] + total[(b - 1) * rows:b * rows])
    carry = jnp.concatenate(carries, axis=0)
    a = jnp.exp(z - sp - later - jnp.concatenate([carry] * (KEY_BLOCK // LANES), axis=1))
    a = jnp.concatenate([jnp.where(new_mask, a[0:rows], 0.0), a[rows:]], axis=0).astype(BF16)
    vn = jnp.concatenate([vn_ref[...], pad], axis=0).astype(BF16)
    acc = jnp.dot(a[0:rows], vn, preferred_element_type=F32)
    for b, pages in enumerate(past, start=1):
        for r, g in enumerate(pages):
            acc += lax.dot_general(a[b * rows:(b + 1) * rows, r * page:(r + 1) * page],
                                   v_pages[g][...].astype(BF16), _NT,
                                   preferred_element_type=F32)
    sel = acc * bd
    out = sel[0:t, :]
    for hd in range(1, HEADS):
        out = out + sel[hd * t:(hd + 1) * t, :]
    o_ref[...] = out


def _attn_sample(q, k_new, v_new, cache_k, cache_v, page_table, bias_rows, bd, w2):
    nb, t, width = q.shape
    n_pages = page_table.shape[1]
    page = cache_k.shape[2]
    assert KEY_BLOCK % page == 0 and n_pages % (KEY_BLOCK // page) == 0
    tok = pl.BlockSpec((None, t, width), lambda b, pt: (b, 0, 0))

    def const(shape):
        return pl.BlockSpec(shape, lambda b, pt: (0,) * len(shape), pipeline_mode=pl.Buffered(1))

    def page_spec(j):
        return pl.BlockSpec((None, width, page), lambda b, pt, j=j: (pt[b * n_pages + j], 0, 0))

    grid_spec = pltpu.PrefetchScalarGridSpec(
        num_scalar_prefetch=1,
        grid=(nb,),
        in_specs=[tok, tok, tok, const(bias_rows.shape), const(bd.shape), const(w2.shape)]
        + [page_spec(j) for j in range(n_pages)] * 2,
        out_specs=tok,
    )
    return pl.pallas_call(
        functools.partial(_attn_sample_kernel, n_pages=n_pages),
        grid_spec=grid_spec,
        out_shape=jax.ShapeDtypeStruct((nb, t, width), F32),
        compiler_params=pltpu.CompilerParams(
            dimension_semantics=("arbitrary",), vmem_limit_bytes=VMEM_LIMIT),
        name="attn_sample",
    )(page_table.reshape(-1), q, k_new, v_new, bias_rows, bd, w2,
      *([cache_k] * n_pages), *([cache_v] * n_pages))


def _mixer_out_kernel(*refs, hist_mode):
    if hist_mode:
        (x_ref, ga_ref, sgb_ref, o_ref, p_ref, wb_ref, wo_ref, gpm_ref, gpf_ref, wup_ref,
         wcf_ref, bcf_ref, wdn_ref, gqf_ref, wple_ref, gpg_ref, wpg_ref, gpp_ref,
         ea_ref, eb_ref, st0_ref, st1_ref, y_ref, up_ref, act_scr, c_scr) = refs
        tail_scr = None
    else:
        (x_ref, ga_ref, sgb_ref, o_ref, p_ref, wb_ref, wo_ref, gpm_ref, gpf_ref, wup_ref,
         wcf_ref, bcf_ref, wdn_ref, gqf_ref, wple_ref, gpg_ref, wpg_ref, gpp_ref,
         y_ref, fst_ref, act_scr, c_scr, tail_scr) = refs
    tm = x_ref.shape[0]
    dff = wdn_ref.shape[0]
    yb = jnp.dot(o_ref[...].astype(BF16), wb_ref[...], preferred_element_type=F32)
    merged = ga_ref[...] + sgb_ref[...] * yb
    x1 = x_ref[...] + _rms(jnp.dot(merged.astype(BF16), wo_ref[...],
                                   preferred_element_type=F32), gpm_ref[...])
    h2 = _rms(x1, gpf_ref[...]).astype(BF16)

    if hist_mode:
        tmod = lax.broadcasted_iota(jnp.int32, (tm, FF_CHUNK), 0) & (SUBLANES - 1)
        ea = ea_ref[...]
        eb = eb_ref[...]
    else:
        @pl.when(pl.program_id(1) == 0)
        def _():
            tail_scr[...] = jnp.zeros(tail_scr.shape, F32)

    def conv_part(lo, stage):
        hi = lo + FF_CHUNK
        up = jnp.dot(h2, wup_ref[:, lo:hi], preferred_element_type=F32)
        if hist_mode:
            hist = _history_rows(ea, eb, st0_ref[:, lo:hi], st1_ref[:, lo:hi]) + (tmod,)
            prev = None
            up_ref[:, lo:hi] = up
        else:
            hist = None
            prev = tail_scr[:, lo:hi]
            tail_scr[:, lo:hi] = up[tm - SUBLANES:tm, :]
            fst_ref[:, lo:hi] = up[tm - (KSIZE - 1):tm, :]
        return _causal_conv3(up, stage, wcf_ref[:, lo:hi], prev, hist) + bcf_ref[:, lo:hi]

    for c in range(dff // FF_CHUNK):
        gate = conv_part(c * FF_CHUNK, c_scr.at[2 * (c % 2)])
        val = conv_part(dff + c * FF_CHUNK, c_scr.at[2 * (c % 2) + 1])
        act_scr[:, c * FF_CHUNK:(c + 1) * FF_CHUNK] = (_twice_gelu_tanh(gate) * val).astype(BF16)
    dn = jnp.dot(act_scr[...], wdn_ref[...], preferred_element_type=F32)
    x2 = x1 + _rms(dn, gqf_ref[...])
    gate = _sigmoid(jnp.dot(_rms(x2, gpg_ref[...]).astype(BF16), wpg_ref[...],
                            preferred_element_type=F32))
    e = jnp.dot(p_ref[...].astype(BF16), wple_ref[...], preferred_element_type=F32) * gate
    y_ref[...] = x2 + _rms(e, gpp_ref[...])


def _mixer_out(x, ga, sgb, o, p, wb, wo, gpm, gpf, wup, wcf, bcf, wdn, gqf, wple, gpg, wpg, gpp,
               hist):
    nb, t, d = x.shape
    hist_mode = hist is not None
    tm = min(HIST_OUT_ROW_TILE if hist_mode else OUT_ROW_TILE, t)
    nt = t // tm
    dff2 = wup.shape[1]

    def row_spec(c):
        return pl.BlockSpec((None, tm, c), lambda b, i: (b, i, 0))

    def const_spec(shape):
        return pl.BlockSpec(shape, lambda b, i: (0,) * len(shape), pipeline_mode=pl.Buffered(1))

    consts = [wb, wo, gpm, gpf, wup, wcf, bcf, wdn, gqf, wple, gpg, wpg, gpp]
    in_specs = ([row_spec(d), row_spec(d), row_spec(d), row_spec(o.shape[2]), row_spec(p.shape[2])]
                + [const_spec(c.shape) for c in consts])
    args = [x, ga, sgb, o, p] + consts
    scratch = [pltpu.VMEM((tm, dff2 // 2), BF16), pltpu.VMEM((4, tm + SUBLANES, FF_CHUNK), F32)]
    if hist_mode:
        ea, eb, st0, st1 = hist
        st_spec = pl.BlockSpec((None, tm // SUBLANES, dff2), lambda b, i: (b, i, 0))
        in_specs += [const_spec(ea.shape), const_spec(eb.shape), st_spec, st_spec]
        args += [ea, eb, st0, st1]
        out_shape = [jax.ShapeDtypeStruct((nb, t, d), F32),
                     jax.ShapeDtypeStruct((nb, t, dff2), F32)]
        out_specs = [row_spec(d), row_spec(dff2)]
    else:
        out_shape = [jax.ShapeDtypeStruct((nb, t, d), F32),
                     jax.ShapeDtypeStruct((nb, KSIZE - 1, dff2), F32)]
        out_specs = [row_spec(d), pl.BlockSpec((None, KSIZE - 1, dff2), lambda b, i: (b, 0, 0))]
        scratch.append(pltpu.VMEM((SUBLANES, dff2), F32))
    return pl.pallas_call(
        functools.partial(_mixer_out_kernel, hist_mode=hist_mode),
        grid=(nb, nt),
        in_specs=in_specs,
        out_specs=out_specs,
        out_shape=out_shape,
        scratch_shapes=scratch,
        compiler_params=pltpu.CompilerParams(
            dimension_semantics=("arbitrary", "arbitrary"), vmem_limit_bytes=VMEM_LIMIT),
        name="mixer_out_hist" if hist_mode else "mixer_out",
    )(*args)


def _suffix_sum_matrix():
    j = lax.broadcasted_iota(jnp.int32, (KEY_BLOCK, KEY_BLOCK + LANES), 0)
    s = lax.broadcasted_iota(jnp.int32, (KEY_BLOCK, KEY_BLOCK + LANES), 1)
    w = jnp.where((j > s) | (s >= KEY_BLOCK), 1.0, 0.0).astype(BF16)
    return jnp.concatenate([w, w], axis=0)


def _history_selectors(tm, t):
    r = lax.broadcasted_iota(jnp.int32, (tm, tm // t), 0)
    c = lax.broadcasted_iota(jnp.int32, (tm, tm // t), 1)
    seq, pos = r // t, r % t
    return (((pos == 0) & (c == seq)).astype(BF16), ((pos == 1) & (c == seq)).astype(BF16))


def kernel(x_prompt, x_sample, cache_k, cache_v, state_conv, state_ffn_conv, page_table, p_prompt, p_sample, g_pre_mix, w_in, w_conv_a, w_a, w_b, sb_bias, w_o, g_post_mix, g_pre_ffn, w_up, w_conv_f, b_conv_f, w_down, g_post_ffn, w_ple, g_ple_gate, w_ple_gate, g_post_ple):
    depth = w_in.shape[0]
    nbp, seq, d = x_prompt.shape
    nbs, dseq, _ = x_sample.shape
    assert dseq == SUBLANES and KSIZE == 3
    n_pool, page = cache_k.shape[1], cache_k.shape[2]
    pages_t = lambda c: c.transpose(0, 2, 3, 1).reshape(n_pool, -1, page)
    rows_s = nbs * dseq
    w2 = _suffix_sum_matrix()
    sel_in = _history_selectors(min(ROW_TILE, rows_s), dseq)
    sel_out = _history_selectors(min(HIST_OUT_ROW_TILE, rows_s), dseq)
    r64 = lax.broadcasted_iota(jnp.int32, (HEADS * dseq, SB_WIDTH), 0) // dseq
    c64 = lax.broadcasted_iota(jnp.int32, (HEADS * dseq, SB_WIDTH), 1) // HEAD_DIM
    bd = (r64 == c64).astype(F32)
    row2 = lambda a: a.reshape(1, -1)

    xp = x_prompt
    xs = x_sample.reshape(1, rows_s, d)
    outs = [[] for _ in range(8)]
    for i in range(depth):
        win, wa, wb, wo = (w.astype(BF16) for w in (w_in[i], w_a[i], w_b[i], w_o[i]))
        wup, wdn, wple, wpg = (w.astype(BF16)
                               for w in (w_up[i], 0.5 * w_down[i], w_ple[i], w_ple_gate[i]))
        tail = (wb, wo, row2(g_post_mix[i]), row2(g_pre_ffn[i]), wup, w_conv_f[i],
                row2(b_conv_f[i]), wdn, row2(g_post_ffn[i]), wple, row2(g_ple_gate[i]), wpg,
                row2(g_post_ple[i]))
        q, k16, vt, kf, vf, ga, sgb, cst = _mixer_in(
            xp, row2(g_pre_mix[i]), win, w_conv_a[i], wa, None)
        o = _attn_prompt(q, k16, vt, sb_bias[i])
        xp, fst = _mixer_out(xp, ga, sgb, o, p_prompt[i], *tail, None)
        outs[0].append(kf.reshape(nbp, seq, HEADS, HEAD_DIM))
        outs[1].append(vf.reshape(nbp, seq, HEADS, HEAD_DIM))
        outs[2].append(cst)
        outs[3].append(fst)
        st_a = (state_conv[i][None, :, 0], state_conv[i][None, :, 1])
        st_f = (state_ffn_conv[i][None, :, 0], state_ffn_conv[i][None, :, 1])
        q, kf, vf, ga, sgb, u = _mixer_in(
            xs, row2(g_pre_mix[i]), win, w_conv_a[i], wa, (*sel_in, *st_a))
        bias_rows = jnp.broadcast_to(
            jnp.repeat(sb_bias[i].astype(F32), dseq)[:, None], (HEADS * dseq, LANES))
        o = _attn_sample(q.reshape(nbs, dseq, -1), kf.reshape(nbs, dseq, -1),
                         vf.reshape(nbs, dseq, -1),
                         pages_t(cache_k[i]), pages_t(cache_v[i]),
                         page_table, bias_rows, bd, w2)
        xs, up = _mixer_out(xs, ga, sgb, o.reshape(1, rows_s, -1),
                            p_sample[i].reshape(1, rows_s, -1), *tail, (*sel_out, *st_f))
        outs[4].append(kf.reshape(nbs, dseq, HEADS, HEAD_DIM))
        outs[5].append(vf.reshape(nbs, dseq, HEADS, HEAD_DIM))
        outs[6].append(u.reshape(nbs, dseq, -1)[:, dseq - (KSIZE - 1):])
        outs[7].append(up.reshape(nbs, dseq, -1)[:, dseq - (KSIZE - 1):])
    stacked = [jnp.stack(o) for o in outs]
    return (xp, xs.reshape(nbs, dseq, d), *stacked)
```
